```python
import math
import jax, jax.numpy as jnp
from jax import lax
import numpy as np

D_MODEL = 1024
BATCH = 4
SEQ = 4096
DEPTH = 2

D_MIX = D_MODEL
D_GROUP = D_MIX // 4
CONV_WIDTH = 31
N_DIFF_HEADS = 4
DIFF_QK_DIM = D_GROUP // N_DIFF_HEADS // 2
DIFF_V_DIM = 2 * DIFF_QK_DIM
N_SG_HEADS = 4
SG_HEAD_DIM = D_GROUP // N_SG_HEADS
CHUNK = 128
POOL_WINDOWS = (2, 4, 8, 16)
N_POOL_GROUPS = len(POOL_WINDOWS)
POOL_GROUP_DIM = D_GROUP // N_POOL_GROUPS
D_FF = 4 * D_MODEL
N_BUCKETS = 32
MAX_DISTANCE = 128
Q_BLOCK = 128
LN_EPS = 1e-5
DEEPNORM_ALPHA = (2.0 * DEPTH) ** 0.25
DEEPNORM_BETA = (8.0 * DEPTH) ** -0.25

SPLIT_SIZES = (2 * D_GROUP,
               N_DIFF_HEADS * DIFF_QK_DIM, N_DIFF_HEADS * DIFF_QK_DIM,
               N_DIFF_HEADS * DIFF_QK_DIM, N_DIFF_HEADS * DIFF_QK_DIM,
               N_DIFF_HEADS * DIFF_V_DIM,
               D_GROUP, D_GROUP,
               D_GROUP)
D_IN = sum(SPLIT_SIZES)
SPLIT_OFFSETS = tuple(int(o) for o in np.cumsum((0,) + SPLIT_SIZES))

kernel_name = "hybrid_conv_diffattn_sgmlp_pool_deepnorm"


def layer_norm(x, g, b):
    xf = x.astype(jnp.float32)
    mu = jnp.mean(xf, axis=-1, keepdims=True)
    var = jnp.mean(jnp.square(xf - mu), axis=-1, keepdims=True)
    y = (xf - mu) * lax.rsqrt(var + LN_EPS) * g.astype(jnp.float32) + b.astype(jnp.float32)
    return y.astype(x.dtype)


def rms_norm(x, g):
    xf = x.astype(jnp.float32)
    y = xf * lax.rsqrt(jnp.mean(jnp.square(xf), axis=-1, keepdims=True) + LN_EPS) * g.astype(jnp.float32)
    return y.astype(x.dtype)


def t5_bucket(q_pos, k_pos):
    n = jnp.maximum(q_pos[:, None] - k_pos[None, :], 0)
    max_exact = N_BUCKETS // 2
    large = max_exact + (jnp.log(jnp.maximum(n, 1).astype(jnp.float32) / max_exact)
                         / math.log(MAX_DISTANCE / max_exact) * (N_BUCKETS - max_exact)).astype(jnp.int32)
    large = jnp.minimum(large, N_BUCKETS - 1)
    return jnp.where(n < max_exact, n, large)


def conformer_conv(a, gate, conv_w, conv_b, ln_g, ln_b, pw_w, pw_b):
    h = a * jax.nn.sigmoid(gate)
    h = lax.conv_general_dilated(h, conv_w[:, None, :].astype(h.dtype), window_strides=(1,),
                                 padding=[(CONV_WIDTH - 1, 0)],
                                 dimension_numbers=('NWC', 'WIO', 'NWC'),
                                 feature_group_count=D_GROUP) + conv_b
    h = jax.nn.silu(layer_norm(h, ln_g, ln_b))
    return h @ pw_w + pw_b


def diff_attention(q1, q2, k1, k2, v, lam, lam_init, rel_bias, norm_g):
    B, S = q1.shape[0], q1.shape[1]
    nblk = S // Q_BLOCK
    scale = DIFF_QK_DIM ** -0.5
    k_pos = jnp.arange(S)

    def to_blocks(t):
        return t.reshape(B, nblk, Q_BLOCK, N_DIFF_HEADS, DIFF_QK_DIM).transpose(1, 0, 2, 3, 4)

    def one_block(args):
        q1b, q2b, i = args
        q_pos = i * Q_BLOCK + jnp.arange(Q_BLOCK)
        bias = rel_bias.astype(jnp.float32)[t5_bucket(q_pos, k_pos)]
        bias = bias.transpose(2, 0, 1)[None]
        mask = (k_pos[None, :] <= q_pos[:, None])[None, None]
        s1 = jnp.einsum('bqhd,bkhd->bhqk', q1b, k1).astype(jnp.float32) * scale + bias
        s2 = jnp.einsum('bqhd,bkhd->bhqk', q2b, k2).astype(jnp.float32) * scale + bias
        p1 = jax.nn.softmax(jnp.where(mask, s1, -1e30), axis=-1)
        p2 = jax.nn.softmax(jnp.where(mask, s2, -1e30), axis=-1)
        attn = (p1 - lam * p2).astype(v.dtype)
        return jnp.einsum('bhqk,bkhd->bqhd', attn, v)

    out = lax.map(one_block, (to_blocks(q1), to_blocks(q2), jnp.arange(nblk)))
    out = out.transpose(1, 0, 2, 3, 4).reshape(B, S, N_DIFF_HEADS, DIFF_V_DIM)
    out = rms_norm(out, norm_g) * (1.0 - lam_init)
    return out.reshape(B, S, N_DIFF_HEADS * DIFF_V_DIM)


def spatial_gating(u, v, ln_g, ln_b, sg_w, sg_b):
    B, S = u.shape[0], u.shape[1]
    nc = S // CHUNK
    u = jax.nn.gelu(u)
    v = layer_norm(jax.nn.gelu(v), ln_g, ln_b)
    vc = v.reshape(B, nc, CHUNK, N_SG_HEADS, SG_HEAD_DIM)
    w = sg_w * jnp.tril(jnp.ones((CHUNK, CHUNK), sg_w.dtype))
    mixed = jnp.einsum('hts,bcshd->bcthd', w, vc) + sg_b.T[None, None, :, :, None]
    return u * mixed.reshape(B, S, D_GROUP)


def multiscale_pool(p, pool_w, pool_scale):
    B, S = p.shape[0], p.shape[1]
    pg = p.reshape(B, S, N_POOL_GROUPS, POOL_GROUP_DIM)
    cs = jnp.cumsum(pg.astype(jnp.float32), axis=1)
    t = jnp.arange(S)
    outs = []
    for g, w in enumerate(POOL_WINDOWS):
        c = cs[:, :, g]
        lag = jnp.pad(c, ((0, 0), (w, 0), (0, 0)))[:, :S]
        cnt = jnp.minimum(t + 1, w).astype(jnp.float32)[None, :, None]
        outs.append((c - lag) / cnt)
    pooled = jnp.stack(outs, axis=2).astype(p.dtype) - pg
    mixed = jnp.einsum('bsgc,gcd->bsgd', pooled, pool_w)
    return mixed.reshape(B, S, D_GROUP) * pool_scale


def setup_inputs(seed: int = 0) -> dict:
    key = jax.random.key(seed)
    ks = jax.random.split(key, 32)
    f32 = jnp.float32
    L = DEPTH

    def nrm(k, shape, scale):
        return jax.random.normal(k, shape, f32) * scale

    return {
        "x": jax.random.normal(ks[0], (BATCH, SEQ, D_MODEL), f32),
        "emb_ln_g": 1.0 + nrm(ks[1], (D_MODEL,), 0.05),
        "emb_ln_b": nrm(ks[2], (D_MODEL,), 0.02),
        "rel_bias": nrm(ks[3], (N_BUCKETS, N_DIFF_HEADS), 0.5),
        "w_in": nrm(ks[4], (L, D_MODEL, D_IN), D_MODEL ** -0.5),
        "conv_w": nrm(ks[5], (L, CONV_WIDTH, D_GROUP), CONV_WIDTH ** -0.5),
        "conv_b": nrm(ks[6], (L, D_GROUP), 0.02),
        "conv_ln_g": 1.0 + nrm(ks[7], (L, D_GROUP), 0.05),
        "conv_ln_b": nrm(ks[8], (L, D_GROUP), 0.02),
        "conv_pw_w": nrm(ks[9], (L, D_GROUP, D_GROUP), D_GROUP ** -0.5),
        "conv_pw_b": nrm(ks[10], (L, D_GROUP), 0.02),
        "lam_q1": nrm(ks[11], (L, DIFF_QK_DIM), 0.1),
        "lam_k1": nrm(ks[12], (L, DIFF_QK_DIM), 0.1),
        "lam_q2": nrm(ks[13], (L, DIFF_QK_DIM), 0.1),
        "lam_k2": nrm(ks[14], (L, DIFF_QK_DIM), 0.1),
        "diff_norm_g": 1.0 + nrm(ks[15], (L, DIFF_V_DIM), 0.05),
        "sg_ln_g": 1.0 + nrm(ks[16], (L, D_GROUP), 0.05),
        "sg_ln_b": nrm(ks[17], (L, D_GROUP), 0.02),
        "sg_w": nrm(ks[18], (L, N_SG_HEADS, CHUNK, CHUNK), 0.5 * CHUNK ** -0.5),
        "sg_b": 1.0 + nrm(ks[19], (L, N_SG_HEADS, CHUNK), 0.1),
        "pool_w": nrm(ks[20], (L, N_POOL_GROUPS, POOL_GROUP_DIM, POOL_GROUP_DIM), POOL_GROUP_DIM ** -0.5),
        "pool_scale": 1.0 + nrm(ks[21], (L, D_GROUP), 0.1),
        "w_out": nrm(ks[22], (L, D_MIX, D_MODEL), DEEPNORM_BETA * D_MIX ** -0.5),
        "ln1_g": 1.0 + nrm(ks[23], (L, D_MODEL), 0.05),
        "ln1_b": nrm(ks[24], (L, D_MODEL), 0.02),
        "w_mlp1": nrm(ks[25], (L, D_MODEL, D_FF), D_MODEL ** -0.5),
        "w_mlp2": nrm(ks[26], (L, D_FF, D_MODEL), DEEPNORM_BETA * D_FF ** -0.5),
        "ln2_g": 1.0 + nrm(ks[27], (L, D_MODEL), 0.05),
        "ln2_b": nrm(ks[28], (L, D_MODEL), 0.02),
    }


def reference(x, emb_ln_g, emb_ln_b, rel_bias, w_in, conv_w, conv_b, conv_ln_g, conv_ln_b,
              conv_pw_w, conv_pw_b, lam_q1, lam_k1, lam_q2, lam_k2, diff_norm_g,
              sg_ln_g, sg_ln_b, sg_w, sg_b, pool_w, pool_scale, w_out,
              ln1_g, ln1_b, w_mlp1, w_mlp2, ln2_g, ln2_b):
    B, S, _ = x.shape
    o = SPLIT_OFFSETS
    h = layer_norm(x, emb_ln_g, emb_ln_b)
    for l in range(DEPTH):
        z = h @ w_in[l]
        za, zq1, zq2, zk1, zk2, zv, zu, zsv, zp = [z[..., o[i]:o[i + 1]] for i in range(len(SPLIT_SIZES))]

        y_conv = conformer_conv(za[..., :D_GROUP], za[..., D_GROUP:], conv_w[l], conv_b[l],
                                conv_ln_g[l], conv_ln_b[l], conv_pw_w[l], conv_pw_b[l])

        lam_init = 0.8 - 0.6 * math.exp(-0.3 * l)
        lam = (jnp.exp(jnp.sum(lam_q1[l].astype(jnp.float32) * lam_k1[l].astype(jnp.float32)))
               - jnp.exp(jnp.sum(lam_q2[l].astype(jnp.float32) * lam_k2[l].astype(jnp.float32)))
               + lam_init)
        qk_shape = (B, S, N_DIFF_HEADS, DIFF_QK_DIM)
        y_diff = diff_attention(zq1.reshape(qk_shape), zq2.reshape(qk_shape),
                                zk1.reshape(qk_shape), zk2.reshape(qk_shape),
                                zv.reshape(B, S, N_DIFF_HEADS, DIFF_V_DIM),
                                lam, lam_init, rel_bias, diff_norm_g[l])

        y_sg = spatial_gating(zu, zsv, sg_ln_g[l], sg_ln_b[l], sg_w[l], sg_b[l])

        y_pool = multiscale_pool(zp, pool_w[l], pool_scale[l])

        mix = jnp.concatenate([y_conv, y_diff, y_sg, y_pool], axis=-1) @ w_out[l]
        h = layer_norm(DEEPNORM_ALPHA * h + mix, ln1_g[l], ln1_b[l])

        ff = jnp.square(jax.nn.relu(h @ w_mlp1[l])) @ w_mlp2[l]
        h = layer_norm(DEEPNORM_ALPHA * h + ff, ln2_g[l], ln2_b[l])
    return h
```

```python
import functools
import math

import jax
import jax.numpy as jnp
import numpy as np
from jax import lax
from jax.experimental import pallas as pl
from jax.experimental.pallas import tpu as pltpu

F32 = jnp.float32
BF16 = jnp.bfloat16

D_MODEL = 1024
DEPTH = 2
D_GROUP = 256
CONV_WIDTH = 31
N_HEADS = 4
QK_DIM = 32
V_DIM = 64
N_SG_HEADS = 4
SG_HEAD_DIM = 64
CHUNK = 128
POOL_WINDOWS = (2, 4, 8, 16)
POOL_GROUP_DIM = 64
D_FF = 4096
N_BUCKETS = 32
MAX_DISTANCE = 128
LN_EPS = 1e-5
ALPHA = (2.0 * DEPTH) ** 0.25
D_IN = 2048
O_A, O_Q, O_K, O_V, O_U, O_SV, O_P = 0, 512, 768, 1024, 1280, 1536, 1792

LANES = 128
VMEM_LIMIT_BYTES = 56 * 1024 * 1024

TM_IN = 512
TM_OUT = 512
TQ = 256
TK = 256
HALO = 32
ROW_CHUNK = 64
V_AUG = 80
N_MAPS = 2 * N_HEADS
MASK_VALUE = -1e30
LOG2E = math.log2(math.e)
Q_SCALE = (QK_DIM ** -0.5) * LOG2E


def _layer_norm(x, g, b):
    mu = jnp.mean(x, axis=-1, keepdims=True)
    xc = x - mu
    var = jnp.mean(xc * xc, axis=-1, keepdims=True)
    return xc * lax.rsqrt(var + LN_EPS) * g + b


def _gelu_tanh(x):
    c = math.sqrt(2.0 / math.pi)
    return 0.5 * x * (1.0 + jnp.tanh(c * (x + 0.044715 * (x * x * x))))


def _sigmoid(x):
    return 1.0 / (1.0 + jnp.exp(-x))


def _mixer_in_kernel(first, x_ref, lng_ref, lnb_ref, w_in_ref, conv_w_ref, conv_b_ref, cln_g_ref,
                     cln_b_ref, pw_w_ref, pw_b_ref, sgln_g_ref, sgln_b_ref, sg_w_ref, sg_bias_ref,
                     pool_w_ref, pool_scale_ref, *rest):
    if first:
        h_ref, q_ref, k_ref, v_ref, ymix_ref, glu_buf, pool_buf = rest
    else:
        q_ref, k_ref, v_ref, ymix_ref, glu_buf, pool_buf = rest
    tm = x_ref.shape[1]
    j = pl.program_id(1)

    x = x_ref[0]
    if first:
        h = _layer_norm(x, lng_ref[...], lnb_ref[...])
        h_ref[0] = h
    else:
        h = x
    hb = h.astype(BF16)

    def proj(lo, hi):
        return jnp.dot(hb, w_in_ref[:, lo:hi], preferred_element_type=F32)

    q_ref[0] = (proj(O_Q, O_K) * Q_SCALE).astype(BF16)
    k_ref[0] = proj(O_K, O_V).astype(BF16)
    v_ref[0] = proj(O_V, O_U).astype(BF16)

    @pl.when(j == 0)
    def _():
        glu_buf[0:HALO, :] = jnp.zeros((HALO, D_GROUP), F32)
        pool_buf[0:HALO, :] = jnp.zeros((HALO, D_GROUP), F32)

    @pl.when(j > 0)
    def _():
        glu_buf[0:HALO, :] = glu_buf[tm:tm + HALO, :]
        pool_buf[0:HALO, :] = pool_buf[tm:tm + HALO, :]

    za = proj(O_A, O_Q)
    glu_buf[HALO:HALO + tm, :] = za[:, :D_GROUP] * _sigmoid(za[:, D_GROUP:])
    conv_b = conv_b_ref[...]
    conv_rows = []
    for r0 in range(0, tm, ROW_CHUNK):
        acc = jnp.broadcast_to(conv_b, (ROW_CHUNK, D_GROUP))
        for tap in range(CONV_WIDTH):
            off = HALO + r0 - (CONV_WIDTH - 1) + tap
            acc = acc + conv_w_ref[tap:tap + 1, :] * glu_buf[off:off + ROW_CHUNK, :]
        conv_rows.append(acc)
    hc = jnp.concatenate(conv_rows, axis=0)
    hc = _layer_norm(hc, cln_g_ref[...], cln_b_ref[...])
    hc = hc * _sigmoid(hc)
    y_conv = jnp.dot(hc.astype(BF16), pw_w_ref[...], preferred_element_type=F32) + pw_b_ref[...]
    ymix_ref[0, :, 0:D_GROUP] = y_conv.astype(BF16)

    gu = _gelu_tanh(proj(O_U, O_SV))
    gv = _layer_norm(_gelu_tanh(proj(O_SV, O_P)), sgln_g_ref[...], sgln_b_ref[...]).astype(BF16)
    lane = lax.broadcasted_iota(jnp.int32, (CHUNK, D_GROUP), 1)
    sg_w = sg_w_ref[...]
    sg_bias = sg_bias_ref[...]
    for c in range(tm // CHUNK):
        vc = gv[c * CHUNK:(c + 1) * CHUNK, :]
        stacked = jnp.concatenate(
            [jnp.where((lane >= hd * SG_HEAD_DIM) & (lane < (hd + 1) * SG_HEAD_DIM), vc,
                       jnp.zeros_like(vc)) for hd in range(N_SG_HEADS)], axis=0)
        mixed = jnp.dot(sg_w, stacked, preferred_element_type=F32) + sg_bias
        ymix_ref[0, c * CHUNK:(c + 1) * CHUNK, D_GROUP:2 * D_GROUP] = (
            gu[c * CHUNK:(c + 1) * CHUNK, :] * mixed).astype(BF16)

    pool_buf[HALO:HALO + tm, :] = proj(O_P, D_IN)
    lane_r = lax.broadcasted_iota(jnp.int32, (ROW_CHUNK, D_GROUP), 1)
    row_r = lax.broadcasted_iota(jnp.int32, (ROW_CHUNK, D_GROUP), 0)
    group = lane_r // POOL_GROUP_DIM
    window = jnp.where(group == 0, POOL_WINDOWS[0],
                       jnp.where(group == 1, POOL_WINDOWS[1],
                                 jnp.where(group == 2, POOL_WINDOWS[2], POOL_WINDOWS[3])))
    pooled_rows = []
    for r0 in range(0, tm, ROW_CHUNK):
        base = HALO + r0
        cur = pool_buf[base:base + ROW_CHUNK, :]
        run = cur
        sums = {}
        for lag in range(1, POOL_WINDOWS[-1]):
            run = run + pool_buf[base - lag:base - lag + ROW_CHUNK, :]
            if lag + 1 in POOL_WINDOWS:
                sums[lag + 1] = run
        total = jnp.where(group == 0, sums[2],
                          jnp.where(group == 1, sums[4], jnp.where(group == 2, sums[8], sums[16])))
        t_pos = j * tm + r0 + row_r
        cnt = jnp.minimum(t_pos + 1, window).astype(F32)
        pooled_rows.append(total / cnt - cur)
    pooled = jnp.concatenate(pooled_rows, axis=0)
    y_pool = jnp.dot(pooled.astype(BF16), pool_w_ref[...], preferred_element_type=F32)
    ymix_ref[0, :, 2 * D_GROUP:3 * D_GROUP] = (y_pool * pool_scale_ref[...]).astype(BF16)


def _const_spec(shape):
    return pl.BlockSpec(shape, lambda *_: (0,) * len(shape))


def _mixer_in(first, x, lng, lnb, w_in, conv_w, conv_b, cln_g, cln_b, pw_w, pw_b, sgln_g, sgln_b,
              sg_w, sg_bias, pool_w, pool_scale):
    B, S, _ = x.shape
    tm = TM_IN
    tok = lambda width: pl.BlockSpec((1, tm, width), lambda b, j: (b, j, 0))
    consts = [lng, lnb, w_in, conv_w, conv_b, cln_g, cln_b, pw_w, pw_b, sgln_g, sgln_b, sg_w,
              sg_bias, pool_w, pool_scale]
    out_shape = [jax.ShapeDtypeStruct((B, S, D_GROUP), BF16),
                 jax.ShapeDtypeStruct((B, S, D_GROUP), BF16),
                 jax.ShapeDtypeStruct((B, S, D_GROUP), BF16),
                 jax.ShapeDtypeStruct((B, S, 3 * D_GROUP), BF16)]
    out_specs = [tok(D_GROUP), tok(D_GROUP), tok(D_GROUP), tok(3 * D_GROUP)]
    if first:
        out_shape = [jax.ShapeDtypeStruct((B, S, D_MODEL), F32)] + out_shape
        out_specs = [tok(D_MODEL)] + out_specs
    return pl.pallas_call(
        functools.partial(_mixer_in_kernel, first),
        grid=(B, S // tm),
        in_specs=[tok(D_MODEL)] + [_const_spec(c.shape) for c in consts],
        out_specs=out_specs,
        out_shape=out_shape,
        scratch_shapes=[pltpu.VMEM((HALO + tm, D_GROUP), F32), pltpu.VMEM((HALO + tm, D_GROUP), F32)],
        compiler_params=pltpu.CompilerParams(dimension_semantics=("parallel", "arbitrary"),
                                             vmem_limit_bytes=VMEM_LIMIT_BYTES),
        name="mixer_in",
    )(x, *consts)


def _diff_attn_kernel(lam_init, q_ref, k_ref, v_ref, bias_ref, lq1_ref, lk1_ref, lq2_ref, lk2_ref,
                      g_ref, o_ref, vt_ref, qm_ref, m_ref, acc_ref):
    i = pl.program_id(1)
    n_kv = k_ref.shape[1] // TK

    @pl.when(i == 0)
    def _():
        row = lax.broadcasted_iota(jnp.int32, (V_AUG - V_DIM, TK), 0)
        ones_rows = jnp.where(row == 0, 1.0, 0.0).astype(BF16)
        for jt in range(n_kv):
            vt = v_ref[0, jt * TK:(jt + 1) * TK, :].astype(F32).T
            for hd in range(N_HEADS):
                vt_ref[jt, hd, 0:V_DIM, :] = vt[hd * V_DIM:(hd + 1) * V_DIM, :].astype(BF16)
                vt_ref[jt, hd, V_DIM:V_AUG, :] = ones_rows

    q = q_ref[0]
    lane = lax.broadcasted_iota(jnp.int32, q.shape, 1)
    for hd in range(N_HEADS):
        for mp in range(2):
            lo = mp * (N_HEADS * QK_DIM) + hd * QK_DIM
            blk = hd * 2 + mp
            qm_ref[blk * TQ:(blk + 1) * TQ, :] = jnp.where((lane >= lo) & (lane < lo + QK_DIM), q,
                                                           jnp.zeros_like(q))

    m_ref[...] = jnp.full(m_ref.shape, MASK_VALUE, F32)
    acc_ref[...] = jnp.zeros(acc_ref.shape, F32)

    def kv_step(jt, bias):
        kt = k_ref[0, pl.ds(pl.multiple_of(jt * TK, TK), TK), :]
        st = lax.dot_general(kt, qm_ref[...], (((1,), (1,)), ((), ())),
                             preferred_element_type=F32)
        if bias is not None:
            st = st + bias
        m_old = m_ref[...]
        m_new = jnp.maximum(m_old, jnp.max(st, axis=0, keepdims=True))
        alpha = jnp.exp2(m_old - m_new)
        p = jnp.exp2(st - m_new).astype(BF16)
        m_ref[...] = m_new
        for hd in range(N_HEADS):
            cols = slice(hd * 2 * TQ, (hd + 1) * 2 * TQ)
            pv = jnp.dot(vt_ref[jt, hd], p[:, cols], preferred_element_type=F32)
            acc_ref[hd] = acc_ref[hd] * alpha[:, cols] + pv

    def far_body(jt, carry):
        kv_step(jt, None)
        return carry

    lax.fori_loop(0, jnp.maximum(i - 1, 0), far_body, 0)

    @pl.when(i > 0)
    def _():
        kv_step(i - 1, bias_ref[1])

    kv_step(i, bias_ref[0])

    lam = (jnp.exp(jnp.sum(lq1_ref[...] * lk1_ref[...], keepdims=True))
           - jnp.exp(jnp.sum(lq2_ref[...] * lk2_ref[...], keepdims=True)) + lam_init)
    outs = []
    for hd in range(N_HEADS):
        a = acc_ref[hd]
        o1 = a[0:V_DIM, 0:TQ] / a[V_DIM:V_DIM + 1, 0:TQ]
        o2 = a[0:V_DIM, TQ:2 * TQ] / a[V_DIM:V_DIM + 1, TQ:2 * TQ]
        o = o1 - lam * o2
        ms = jnp.mean(o * o, axis=0, keepdims=True)
        outs.append(o * lax.rsqrt(ms + LN_EPS) * g_ref[...] * (1.0 - lam_init))
    o_ref[0] = jnp.concatenate(outs, axis=0).T.astype(BF16)


def _diff_attn(lam_init, q, k, v, bias, lq1, lk1, lq2, lk2, g_col):
    B, S, _ = q.shape
    n_kv = S // TK
    return pl.pallas_call(
        functools.partial(_diff_attn_kernel, lam_init),
        grid=(B, S // TQ),
        in_specs=[pl.BlockSpec((1, TQ, D_GROUP), lambda b, i: (b, i, 0)),
                  pl.BlockSpec((1, S, D_GROUP), lambda b, i: (b, 0, 0)),
                  pl.BlockSpec((1, S, D_GROUP), lambda b, i: (b, 0, 0)),
                  _const_spec(bias.shape), _const_spec(lq1.shape), _const_spec(lk1.shape),
                  _const_spec(lq2.shape), _const_spec(lk2.shape), _const_spec(g_col.shape)],
        out_specs=pl.BlockSpec((1, TQ, D_GROUP), lambda b, i: (b, i, 0)),
        out_shape=jax.ShapeDtypeStruct((B, S, D_GROUP), BF16),
        scratch_shapes=[pltpu.VMEM((n_kv, N_HEADS, V_AUG, TK), BF16),
                        pltpu.VMEM((N_MAPS * TQ, D_GROUP), BF16),
                        pltpu.VMEM((1, N_MAPS * TQ), F32),
                        pltpu.VMEM((N_HEADS, V_AUG, 2 * TQ), F32)],
        compiler_params=pltpu.CompilerParams(dimension_semantics=("parallel", "arbitrary"),
                                             vmem_limit_bytes=VMEM_LIMIT_BYTES),
        name="diff_attn",
    )(q, k, v, bias, lq1, lk1, lq2, lk2, g_col)


def _mixer_out_kernel(h_ref, ymix_ref, ydiff_ref, w_out_ref, ln1_g_ref, ln1_b_ref, w1_ref, w2_ref,
                      ln2_g_ref, ln2_b_ref, o_ref):
    h = h_ref[...]
    ymix = ymix_ref[...]
    mix = (jnp.dot(ymix[:, 0:D_GROUP], w_out_ref[0:D_GROUP, :], preferred_element_type=F32)
           + jnp.dot(ydiff_ref[...], w_out_ref[D_GROUP:2 * D_GROUP, :], preferred_element_type=F32)
           + jnp.dot(ymix[:, D_GROUP:3 * D_GROUP], w_out_ref[2 * D_GROUP:4 * D_GROUP, :],
                     preferred_element_type=F32))
    h1 = _layer_norm(ALPHA * h + mix, ln1_g_ref[...], ln1_b_ref[...])
    a = jnp.dot(h1.astype(BF16), w1_ref[...], preferred_element_type=F32)
    a = jnp.maximum(a, 0.0)
    a = (a * a).astype(BF16)
    ff = jnp.dot(a, w2_ref[...], preferred_element_type=F32)
    o_ref[...] = _layer_norm(ALPHA * h1 + ff, ln2_g_ref[...], ln2_b_ref[...])


def _mixer_out(h, ymix, ydiff, w_out, ln1_g, ln1_b, w1, w2, ln2_g, ln2_b):
    T = h.shape[0]
    tm = TM_OUT
    tok = lambda width: pl.BlockSpec((tm, width), lambda t: (t, 0))
    consts = [w_out, ln1_g, ln1_b, w1, w2, ln2_g, ln2_b]
    single = lambda c: pl.BlockSpec(c.shape, lambda t: (0, 0), pipeline_mode=pl.Buffered(1))
    return pl.pallas_call(
        _mixer_out_kernel,
        grid=(T // tm,),
        in_specs=[tok(D_MODEL), tok(3 * D_GROUP), tok(D_GROUP)] + [single(c) for c in consts],
        out_specs=tok(D_MODEL),
        out_shape=jax.ShapeDtypeStruct((T, D_MODEL), F32),
        compiler_params=pltpu.CompilerParams(dimension_semantics=("parallel",),
                                             vmem_limit_bytes=VMEM_LIMIT_BYTES),
        name="mixer_out",
    )(h, ymix, ydiff, *consts)


def _t5_bucket(n):
    max_exact = N_BUCKETS // 2
    large = max_exact + (jnp.log(jnp.maximum(n, 1).astype(F32) / max_exact)
                         / math.log(MAX_DISTANCE / max_exact) * (N_BUCKETS - max_exact)).astype(jnp.int32)
    large = jnp.minimum(large, N_BUCKETS - 1)
    return jnp.where(n < max_exact, n, large)


def _bias_tiles(rel_bias):
    assert TK >= MAX_DISTANCE, "tiles two or more behind the diagonal must see one constant bias"
    dist = jnp.arange(TQ + TK)
    vals = rel_bias.astype(F32)[_t5_bucket(dist)]
    far = rel_bias.astype(F32)[N_BUCKETS - 1]
    vals = (vals - far[None, :]) * LOG2E
    d_diag = jnp.arange(TQ)[None, :] - jnp.arange(TK)[:, None]
    diag = jnp.where((d_diag >= 0)[..., None], vals[jnp.maximum(d_diag, 0)], MASK_VALUE)
    prev = vals[d_diag + TK]
    tiles = jnp.stack([diag, prev])
    tiles = jnp.transpose(tiles, (0, 1, 3, 2))
    tiles = jnp.broadcast_to(tiles[:, :, :, None, :], (2, TK, N_HEADS, 2, TQ))
    return tiles.reshape(2, TK, N_MAPS * TQ)


def _block_diag(w):
    g, c, d = w.shape
    out = jnp.zeros((g * c, g * d), w.dtype)
    for i in range(g):
        out = out.at[i * c:(i + 1) * c, i * d:(i + 1) * d].set(w[i])
    return out


def kernel(x, emb_ln_g, emb_ln_b, rel_bias, w_in, conv_w, conv_b, conv_ln_g, conv_ln_b, conv_pw_w,
           conv_pw_b, lam_q1, lam_k1, lam_q2, lam_k2, diff_norm_g, sg_ln_g, sg_ln_b, sg_w, sg_b, pool_w,
           pool_scale, w_out, ln1_g, ln1_b, w_mlp1, w_mlp2, ln2_g, ln2_b):
    B, S, _ = x.shape
    assert S % TM_IN == 0 and (B * S) % TM_OUT == 0 and S % TQ == 0 and TQ == TK
    row = lambda p: p.reshape(1, -1).astype(F32)
    bias = _bias_tiles(rel_bias)
    tril = jnp.tril(jnp.ones((CHUNK, CHUNK), F32))

    h = x
    for l in range(DEPTH):
        lam_init = 0.8 - 0.6 * math.exp(-0.3 * l)
        sg_w_cat = jnp.transpose(sg_w[l] * tril, (1, 0, 2)).reshape(CHUNK, N_SG_HEADS * CHUNK)
        sg_bias = jnp.repeat(sg_b[l].T, SG_HEAD_DIM, axis=1)
        outs = _mixer_in(
            l == 0, h, row(emb_ln_g), row(emb_ln_b), w_in[l].astype(BF16), conv_w[l].astype(F32),
            row(conv_b[l]), row(conv_ln_g[l]), row(conv_ln_b[l]), conv_pw_w[l].astype(BF16),
            row(conv_pw_b[l]), row(sg_ln_g[l]), row(sg_ln_b[l]), sg_w_cat.astype(BF16),
            sg_bias.astype(F32), _block_diag(pool_w[l]).astype(BF16), row(pool_scale[l]))
        if l == 0:
            h, q, k, v, ymix = outs
        else:
            q, k, v, ymix = outs
        ydiff = _diff_attn(lam_init, q, k, v, bias, row(lam_q1[l]), row(lam_k1[l]), row(lam_q2[l]),
                           row(lam_k2[l]), diff_norm_g[l].reshape(-1, 1).astype(F32))
        h = _mixer_out(h.reshape(B * S, D_MODEL), ymix.reshape(B * S, 3 * D_GROUP),
                       ydiff.reshape(B * S, D_GROUP), w_out[l].astype(BF16), row(ln1_g[l]),
                       row(ln1_b[l]), w_mlp1[l].astype(BF16), w_mlp2[l].astype(BF16), row(ln2_g[l]),
                       row(ln2_b[l])).reshape(B, S, D_MODEL)
    return h
```

```python
import functools
import math

import jax
import jax.numpy as jnp
import numpy as np
from jax import lax
from jax.experimental import pallas as pl
from jax.experimental.pallas import tpu as pltpu

F32 = jnp.float32
BF16 = jnp.bfloat16

D_MODEL = 1024
DEPTH = 2
D_GROUP = 256
CONV_WIDTH = 31
N_HEADS = 4
QK_DIM = 32
V_DIM = 64
N_SG_HEADS = 4
SG_HEAD_DIM = 64
CHUNK = 128
POOL_WINDOWS = (2, 4, 8, 16)
POOL_GROUP_DIM = 64
D_FF = 4096
N_BUCKETS = 32
MAX_DISTANCE = 128
LN_EPS = 1e-5
ALPHA = (2.0 * DEPTH) ** 0.25
D_IN = 2048
O_A, O_Q, O_K, O_V, O_U, O_SV, O_P = 0, 512, 768, 1024, 1280, 1536, 1792

LANES = 128
SUBLANES = 8
VMEM_LIMIT_BYTES = 56 * 1024 * 1024

TM_IN = 512
TM_OUT = 512
TQ = 256
TK = 256
HALO = 32
ROW_CHUNK = 64
V_AUG = 80
N_MAPS = 2 * N_HEADS
MASK_VALUE = -1e30
LOG2E = math.log2(math.e)
Q_SCALE = (QK_DIM ** -0.5) * LOG2E


def _layer_norm(x, g, b):
    mu = jnp.mean(x, axis=-1, keepdims=True)
    xc = x - mu
    var = jnp.mean(xc * xc, axis=-1, keepdims=True)
    return xc * lax.rsqrt(var + LN_EPS) * g + b


def _gelu_tanh(x):
    c = math.sqrt(2.0 / math.pi)
    return 0.5 * x * (1.0 + jnp.tanh(c * (x + 0.044715 * (x * x * x))))


def _sigmoid(x):
    return 1.0 / (1.0 + jnp.exp(-x))


def _mixer_in_kernel(first, x_ref, lng_ref, lnb_ref, w_in_ref, conv_w_ref, conv_b_ref, cln_g_ref,
                     cln_b_ref, pw_w_ref, pw_b_ref, sgln_g_ref, sgln_b_ref, sg_w_ref, sg_bias_ref,
                     pool_w_ref, pool_scale_ref, *rest):
    if first:
        h_ref, q_ref, k_ref, v_ref, ymix_ref, glu_buf, shift_buf, pool_buf, win_buf = rest
    else:
        q_ref, k_ref, v_ref, ymix_ref, glu_buf, shift_buf, pool_buf, win_buf = rest
    tm = x_ref.shape[1]
    j = pl.program_id(1)

    x = x_ref[0]
    if first:
        h = _layer_norm(x, lng_ref[...], lnb_ref[...])
        h_ref[0] = h
    else:
        h = x
    hb = h.astype(BF16)

    def proj(lo, hi):
        return jnp.dot(hb, w_in_ref[:, lo:hi], preferred_element_type=F32)

    q_ref[0] = (proj(O_Q, O_K) * Q_SCALE).astype(BF16)
    k_ref[0] = proj(O_K, O_V).astype(BF16)
    v_ref[0] = proj(O_V, O_U).astype(BF16)

    @pl.when(j == 0)
    def _():
        glu_buf[0:HALO, :] = jnp.zeros((HALO, D_GROUP), F32)
        pool_buf[0:HALO, :] = jnp.zeros((HALO, D_GROUP), F32)

    @pl.when(j > 0)
    def _():
        glu_buf[0:HALO, :] = glu_buf[tm:tm + HALO, :]
        pool_buf[0:HALO, :] = pool_buf[tm:tm + HALO, :]

    za = proj(O_A, O_Q)
    glu_buf[HALO:HALO + tm, :] = za[:, :D_GROUP] * _sigmoid(za[:, D_GROUP:])
    n_shift = HALO + tm - SUBLANES
    for s in range(1, SUBLANES):
        shift_buf[s - 1] = glu_buf[s:s + n_shift, :]
    conv_b = conv_b_ref[...]
    conv_rows = []
    for r0 in range(0, tm, ROW_CHUNK):
        acc = jnp.broadcast_to(conv_b, (ROW_CHUNK, D_GROUP))
        for tap in range(CONV_WIDTH):
            off = HALO + r0 - (CONV_WIDTH - 1) + tap
            base, s = off - off % SUBLANES, off % SUBLANES
            if s == 0:
                rows = glu_buf[base:base + ROW_CHUNK, :]
            else:
                assert base + ROW_CHUNK <= n_shift
                rows = shift_buf[s - 1, base:base + ROW_CHUNK, :]
            acc = acc + conv_w_ref[tap:tap + 1, :] * rows
        conv_rows.append(acc)
    hc = jnp.concatenate(conv_rows, axis=0)
    hc = _layer_norm(hc, cln_g_ref[...], cln_b_ref[...])
    hc = hc * _sigmoid(hc)
    y_conv = jnp.dot(hc.astype(BF16), pw_w_ref[...], preferred_element_type=F32) + pw_b_ref[...]
    ymix_ref[0, :, 0:D_GROUP] = y_conv.astype(BF16)

    gu = _gelu_tanh(proj(O_U, O_SV))
    gv = _layer_norm(_gelu_tanh(proj(O_SV, O_P)), sgln_g_ref[...], sgln_b_ref[...]).astype(BF16)
    lane = lax.broadcasted_iota(jnp.int32, (CHUNK, D_GROUP), 1)
    sg_w = sg_w_ref[...]
    sg_bias = sg_bias_ref[...]
    for c in range(tm // CHUNK):
        vc = gv[c * CHUNK:(c + 1) * CHUNK, :]
        stacked = jnp.concatenate(
            [jnp.where((lane >= hd * SG_HEAD_DIM) & (lane < (hd + 1) * SG_HEAD_DIM), vc,
                       jnp.zeros_like(vc)) for hd in range(N_SG_HEADS)], axis=0)
        mixed = jnp.dot(sg_w, stacked, preferred_element_type=F32) + sg_bias
        ymix_ref[0, c * CHUNK:(c + 1) * CHUNK, D_GROUP:2 * D_GROUP] = (
            gu[c * CHUNK:(c + 1) * CHUNK, :] * mixed).astype(BF16)

    pool_buf[HALO:HALO + tm, :] = proj(O_P, D_IN)
    assert POOL_WINDOWS == (2, 4, 8, 16) and HALO == 32
    n = HALO + tm
    win_buf[0, 8:n, :] = pool_buf[8:n, :] + pool_buf[7:n - 1, :]
    win_buf[1, 16:n, :] = win_buf[0, 16:n, :] + win_buf[0, 14:n - 2, :]
    win_buf[2, 24:n, :] = win_buf[1, 24:n, :] + win_buf[1, 20:n - 4, :]
    sum16 = win_buf[2, HALO:n, :] + win_buf[2, HALO - 8:n - 8, :]
    lane_r = lax.broadcasted_iota(jnp.int32, (tm, D_GROUP), 1)
    row_r = lax.broadcasted_iota(jnp.int32, (tm, D_GROUP), 0)
    group = lane_r // POOL_GROUP_DIM
    window = jnp.where(group == 0, POOL_WINDOWS[0],
                       jnp.where(group == 1, POOL_WINDOWS[1],
                                 jnp.where(group == 2, POOL_WINDOWS[2], POOL_WINDOWS[3])))
    total = jnp.where(group == 0, win_buf[0, HALO:n, :],
                      jnp.where(group == 1, win_buf[1, HALO:n, :],
                                jnp.where(group == 2, win_buf[2, HALO:n, :], sum16)))
    cnt = jnp.minimum(j * tm + row_r + 1, window).astype(F32)
    pooled = total / cnt - pool_buf[HALO:n, :]
    y_pool = jnp.dot(pooled.astype(BF16), pool_w_ref[...], preferred_element_type=F32)
    ymix_ref[0, :, 2 * D_GROUP:3 * D_GROUP] = (y_pool * pool_scale_ref[...]).astype(BF16)


def _const_spec(shape):
    return pl.BlockSpec(shape, lambda *_: (0,) * len(shape))


def _mixer_in(first, x, lng, lnb, w_in, conv_w, conv_b, cln_g, cln_b, pw_w, pw_b, sgln_g, sgln_b,
              sg_w, sg_bias, pool_w, pool_scale):
    B, S, _ = x.shape
    tm = TM_IN
    tok = lambda width: pl.BlockSpec((1, tm, width), lambda b, j: (b, j, 0))
    consts = [lng, lnb, w_in, conv_w, conv_b, cln_g, cln_b, pw_w, pw_b, sgln_g, sgln_b, sg_w,
              sg_bias, pool_w, pool_scale]
    out_shape = [jax.ShapeDtypeStruct((B, S, D_GROUP), BF16),
                 jax.ShapeDtypeStruct((B, S, D_GROUP), BF16),
                 jax.ShapeDtypeStruct((B, S, D_GROUP), BF16),
                 jax.ShapeDtypeStruct((B, S, 3 * D_GROUP), BF16)]
    out_specs = [tok(D_GROUP), tok(D_GROUP), tok(D_GROUP), tok(3 * D_GROUP)]
    if first:
        out_shape = [jax.ShapeDtypeStruct((B, S, D_MODEL), F32)] + out_shape
        out_specs = [tok(D_MODEL)] + out_specs
    return pl.pallas_call(
        functools.partial(_mixer_in_kernel, first),
        grid=(B, S // tm),
        in_specs=[tok(D_MODEL)] + [_const_spec(c.shape) for c in consts],
        out_specs=out_specs,
        out_shape=out_shape,
        scratch_shapes=[pltpu.VMEM((HALO + tm, D_GROUP), F32),
                        pltpu.VMEM((SUBLANES - 1, HALO + tm - SUBLANES, D_GROUP), F32),
                        pltpu.VMEM((HALO + tm, D_GROUP), F32),
                        pltpu.VMEM((3, HALO + tm, D_GROUP), F32)],
        compiler_params=pltpu.CompilerParams(dimension_semantics=("parallel", "arbitrary"),
                                             vmem_limit_bytes=VMEM_LIMIT_BYTES),
        name="mixer_in",
    )(x, *consts)


def _diff_attn_kernel(lam_init, q_ref, k_ref, v_ref, bias_ref, lq1_ref, lk1_ref, lq2_ref, lk2_ref,
                      g_ref, o_ref, vt_ref, qm_ref, s_ref, m_ref, a_ref, acc_ref):
    i = pl.program_id(1)
    n_kv = k_ref.shape[1] // TK

    @pl.when(i == 0)
    def _():
        row = lax.broadcasted_iota(jnp.int32, (V_AUG - V_DIM, TK), 0)
        ones_rows = jnp.where(row == 0, 1.0, 0.0).astype(BF16)
        for jt in range(n_kv):
            vt = v_ref[0, jt * TK:(jt + 1) * TK, :].astype(F32).T
            for hd in range(N_HEADS):
                vt_ref[jt, hd, 0:V_DIM, :] = vt[hd * V_DIM:(hd + 1) * V_DIM, :].astype(BF16)
                vt_ref[jt, hd, V_DIM:V_AUG, :] = ones_rows

    q = q_ref[0].astype(F32)
    row = lax.broadcasted_iota(jnp.int32, (LANES, TQ), 0)
    for mp in range(2):
        qt = q[:, mp * LANES:(mp + 1) * LANES].T
        for hd in range(N_HEADS):
            keep = (row >= hd * QK_DIM) & (row < (hd + 1) * QK_DIM)
            qm_ref[hd * 2 + mp] = jnp.where(keep, qt, 0.0).astype(BF16)

    m_ref[1] = jnp.full(m_ref.shape[1:], MASK_VALUE, F32)
    acc_ref[...] = jnp.zeros(acc_ref.shape, F32)

    def scores_into(slot, t, bias, hd):
        rows = pl.ds(pl.multiple_of(t * TK, TK), TK)
        for mp in range(2):
            blk = hd * 2 + mp
            cols = slice(blk * TQ, (blk + 1) * TQ)
            st = jnp.dot(k_ref[0, rows, mp * LANES:(mp + 1) * LANES], qm_ref[blk],
                         preferred_element_type=F32)
            if bias is not None:
                st = st + bias_ref[bias, :, hd * TQ:(hd + 1) * TQ]
            s_ref[slot, :, cols] = st
            m_old = m_ref[1 - slot, :, cols]
            m_new = jnp.maximum(m_old, jnp.max(st, axis=0, keepdims=True))
            m_ref[slot, :, cols] = m_new
            a_ref[slot, :, cols] = jnp.exp2(m_old - m_new)

    def values_from(slot, t, hd):
        cols = slice(hd * 2 * TQ, (hd + 1) * 2 * TQ)
        p = jnp.exp2(s_ref[slot, :, cols] - m_ref[slot, :, cols]).astype(BF16)
        pv = jnp.dot(vt_ref[t, hd], p, preferred_element_type=F32)
        acc_ref[hd] = acc_ref[hd] * a_ref[slot, :, cols] + pv

    def stage(score_args, value_args):
        for hd in range(N_HEADS):
            if score_args is not None:
                scores_into(*score_args, hd)
            if value_args is not None:
                values_from(*value_args, hd)

    stage((0, i, 0), None)

    @pl.when(i >= 1)
    def _():
        stage((1, i - 1, 1), (0, i))

    def far_pair(k2, carry):
        t = i - 2 - 2 * k2
        stage((0, t, None), (1, t + 1))
        stage((1, t - 1, None), (0, t))
        return carry

    n_far = jnp.maximum(i - 1, 0)
    lax.fori_loop(0, n_far // 2, far_pair, 0)

    @pl.when((i >= 2) & (i % 2 == 0))
    def _():
        stage((0, 0, None), (1, 1))

    @pl.when(i % 2 == 0)
    def _():
        stage(None, (0, 0))

    @pl.when(i % 2 == 1)
    def _():
        stage(None, (1, 0))

    lam = (jnp.exp(jnp.sum(lq1_ref[...] * lk1_ref[...], keepdims=True))
           - jnp.exp(jnp.sum(lq2_ref[...] * lk2_ref[...], keepdims=True)) + lam_init)
    outs = []
    for hd in range(N_HEADS):
        a = acc_ref[hd]
        o1 = a[0:V_DIM, 0:TQ] / a[V_DIM:V_DIM + 1, 0:TQ]
        o2 = a[0:V_DIM, TQ:2 * TQ] / a[V_DIM:V_DIM + 1, TQ:2 * TQ]
        o = o1 - lam * o2
        ms = jnp.mean(o * o, axis=0, keepdims=True)
        outs.append(o * lax.rsqrt(ms + LN_EPS) * g_ref[...] * (1.0 - lam_init))
    o_ref[0] = jnp.concatenate(outs, axis=0).T.astype(BF16)


def _diff_attn(lam_init, q, k, v, bias, lq1, lk1, lq2, lk2, g_col):
    B, S, _ = q.shape
    n_kv = S // TK
    return pl.pallas_call(
        functools.partial(_diff_attn_kernel, lam_init),
        grid=(B, S // TQ),
        in_specs=[pl.BlockSpec((1, TQ, D_GROUP), lambda b, i: (b, i, 0)),
                  pl.BlockSpec((1, S, D_GROUP), lambda b, i: (b, 0, 0)),
                  pl.BlockSpec((1, S, D_GROUP), lambda b, i: (b, 0, 0)),
                  _const_spec(bias.shape), _const_spec(lq1.shape), _const_spec(lk1.shape),
                  _const_spec(lq2.shape), _const_spec(lk2.shape), _const_spec(g_col.shape)],
        out_specs=pl.BlockSpec((1, TQ, D_GROUP), lambda b, i: (b, i, 0)),
        out_shape=jax.ShapeDtypeStruct((B, S, D_GROUP), BF16),
        scratch_shapes=[pltpu.VMEM((n_kv, N_HEADS, V_AUG, TK), BF16),
                        pltpu.VMEM((N_MAPS, LANES, TQ), BF16),
                        pltpu.VMEM((2, TK, N_MAPS * TQ), F32),
                        pltpu.VMEM((2, 1, N_MAPS * TQ), F32),
                        pltpu.VMEM((2, 1, N_MAPS * TQ), F32),
                        pltpu.VMEM((N_HEADS, V_AUG, 2 * TQ), F32)],
        compiler_params=pltpu.CompilerParams(dimension_semantics=("parallel", "arbitrary"),
                                             vmem_limit_bytes=VMEM_LIMIT_BYTES),
        name="diff_attn",
    )(q, k, v, bias, lq1, lk1, lq2, lk2, g_col)


def _mixer_out_kernel(h_ref, ymix_ref, ydiff_ref, w_out_ref, ln1_g_ref, ln1_b_ref, w1_ref, w2_ref,
                      ln2_g_ref, ln2_b_ref, o_ref):
    h = h_ref[...]
    ymix = ymix_ref[...]
    mix = (jnp.dot(ymix[:, 0:D_GROUP], w_out_ref[0:D_GROUP, :], preferred_element_type=F32)
           + jnp.dot(ydiff_ref[...], w_out_ref[D_GROUP:2 * D_GROUP, :], preferred_element_type=F32)
           + jnp.dot(ymix[:, D_GROUP:3 * D_GROUP], w_out_ref[2 * D_GROUP:4 * D_GROUP, :],
                     preferred_element_type=F32))
    h1 = _layer_norm(ALPHA * h + mix, ln1_g_ref[...], ln1_b_ref[...])
    a = jnp.dot(h1.astype(BF16), w1_ref[...], preferred_element_type=F32)
    a = jnp.maximum(a, 0.0)
    a = (a * a).astype(BF16)
    ff = jnp.dot(a, w2_ref[...], preferred_element_type=F32)
    o_ref[...] = _layer_norm(ALPHA * h1 + ff, ln2_g_ref[...], ln2_b_ref[...])


def _mixer_out(h, ymix, ydiff, w_out, ln1_g, ln1_b, w1, w2, ln2_g, ln2_b):
    T = h.shape[0]
    tm = TM_OUT
    tok = lambda width: pl.BlockSpec((tm, width), lambda t: (t, 0))
    consts = [w_out, ln1_g, ln1_b, w1, w2, ln2_g, ln2_b]
    single = lambda c: pl.BlockSpec(c.shape, lambda t: (0, 0), pipeline_mode=pl.Buffered(1))
    return pl.pallas_call(
        _mixer_out_kernel,
        grid=(T // tm,),
        in_specs=[tok(D_MODEL), tok(3 * D_GROUP), tok(D_GROUP)] + [single(c) for c in consts],
        out_specs=tok(D_MODEL),
        out_shape=jax.ShapeDtypeStruct((T, D_MODEL), F32),
        compiler_params=pltpu.CompilerParams(dimension_semantics=("parallel",),
                                             vmem_limit_bytes=VMEM_LIMIT_BYTES),
        name="mixer_out",
    )(h, ymix, ydiff, *consts)


def _t5_bucket(n):
    max_exact = N_BUCKETS // 2
    large = max_exact + (jnp.log(jnp.maximum(n, 1).astype(F32) / max_exact)
                         / math.log(MAX_DISTANCE / max_exact) * (N_BUCKETS - max_exact)).astype(jnp.int32)
    large = jnp.minimum(large, N_BUCKETS - 1)
    return jnp.where(n < max_exact, n, large)


def _bias_tiles(rel_bias):
    assert TK >= MAX_DISTANCE, "tiles two or more behind the diagonal must see one constant bias"
    n = 2 * (TQ + TK)
    dist = jnp.arange(TQ + TK)
    onehot = (_t5_bucket(dist)[:, None] == jnp.arange(N_BUCKETS)[None, :]).astype(F32)
    rb = rel_bias.astype(F32)
    vals = jnp.sum(onehot[:, :, None] * (rb - rb[N_BUCKETS - 1])[None, :, :], axis=1) * LOG2E
    table = jnp.concatenate([vals, jnp.full((n - TQ - TK, N_HEADS), MASK_VALUE, F32)], axis=0).T
    sheared = jnp.tile(table, (1, TK))[:, :TK * (n - 1)].reshape(N_HEADS, TK, n - 1)
    tiles = jnp.stack([sheared[:, :, 0:TQ], sheared[:, :, TK:TK + TQ]])
    return jnp.transpose(tiles, (0, 2, 1, 3)).reshape(2, TK, N_HEADS * TQ)


def _block_diag(w):
    g, c, d = w.shape
    out = jnp.zeros((g * c, g * d), w.dtype)
    for i in range(g):
        out = out.at[i * c:(i + 1) * c, i * d:(i + 1) * d].set(w[i])
    return out


def kernel(x, emb_ln_g, emb_ln_b, rel_bias, w_in, conv_w, conv_b, conv_ln_g, conv_ln_b, conv_pw_w,
           conv_pw_b, lam_q1, lam_k1, lam_q2, lam_k2, diff_norm_g, sg_ln_g, sg_ln_b, sg_w, sg_b, pool_w,
           pool_scale, w_out, ln1_g, ln1_b, w_mlp1, w_mlp2, ln2_g, ln2_b):
    B, S, _ = x.shape
    assert S % TM_IN == 0 and (B * S) % TM_OUT == 0 and S % TQ == 0 and TQ == TK
    row = lambda p: p.reshape(1, -1).astype(F32)
    bias = _bias_tiles(rel_bias)
    tril = jnp.tril(jnp.ones((CHUNK, CHUNK), F32))

    h = x
    for l in range(DEPTH):
        lam_init = 0.8 - 0.6 * math.exp(-0.3 * l)
        sg_w_cat = jnp.transpose(sg_w[l] * tril, (1, 0, 2)).reshape(CHUNK, N_SG_HEADS * CHUNK)
        sg_bias = jnp.repeat(sg_b[l].T, SG_HEAD_DIM, axis=1)
        outs = _mixer_in(
            l == 0, h, row(emb_ln_g), row(emb_ln_b), w_in[l].astype(BF16), conv_w[l].astype(F32),
            row(conv_b[l]), row(conv_ln_g[l]), row(conv_ln_b[l]), conv_pw_w[l].astype(BF16),
            row(conv_pw_b[l]), row(sg_ln_g[l]), row(sg_ln_b[l]), sg_w_cat.astype(BF16),
            sg_bias.astype(F32), _block_diag(pool_w[l]).astype(BF16), row(pool_scale[l]))
        if l == 0:
            h, q, k, v, ymix = outs
        else:
            q, k, v, ymix = outs
        ydiff = _diff_attn(lam_init, q, k, v, bias, row(lam_q1[l]), row(lam_k1[l]), row(lam_q2[l]),
                           row(lam_k2[l]), diff_norm_g[l].reshape(-1, 1).astype(F32))
        h = _mixer_out(h.reshape(B * S, D_MODEL), ymix.reshape(B * S, 3 * D_GROUP),
                       ydiff.reshape(B * S, D_GROUP), w_out[l].astype(BF16), row(ln1_g[l]),
                       row(ln1_b[l]), w_mlp1[l].astype(BF16), w_mlp2[l].astype(BF16), row(ln2_g[l]),
                       row(ln2_b[l])).reshape(B, S, D_MODEL)
    return h
```

```python
import functools
import math

import jax
import jax.numpy as jnp
import numpy as np
from jax import lax
from jax.experimental import pallas as pl
from jax.experimental.pallas import tpu as pltpu

F32 = jnp.float32
BF16 = jnp.bfloat16

D_MODEL = 1024
DEPTH = 2
D_GROUP = 256
CONV_WIDTH = 31
N_HEADS = 4
QK_DIM = 32
V_DIM = 64
N_SG_HEADS = 4
SG_HEAD_DIM = 64
CHUNK = 128
POOL_WINDOWS = (2, 4, 8, 16)
POOL_GROUP_DIM = 64
D_FF = 4096
N_BUCKETS = 32
MAX_DISTANCE = 128
LN_EPS = 1e-5
ALPHA = (2.0 * DEPTH) ** 0.25
D_IN = 2048
O_A, O_Q, O_K, O_V, O_U, O_SV, O_P = 0, 512, 768, 1024, 1280, 1536, 1792

LANES = 128
SUBLANES = 8
VMEM_LIMIT_BYTES = 56 * 1024 * 1024

TM_IN = 512
TM_OUT = 512
TQ = 256
TK = 256
HALO = 32
ROW_CHUNK = 64
V_AUG = 80
N_MAPS = 2 * N_HEADS
MASK_VALUE = -1e30
LOG2E = math.log2(math.e)
Q_SCALE = (QK_DIM ** -0.5) * LOG2E


def _layer_norm(x, g, b):
    mu = jnp.mean(x, axis=-1, keepdims=True)
    xc = x - mu
    var = jnp.mean(xc * xc, axis=-1, keepdims=True)
    return xc * lax.rsqrt(var + LN_EPS) * g + b


def _gelu_tanh(x):
    c = math.sqrt(2.0 / math.pi)
    return 0.5 * x * (1.0 + jnp.tanh(c * (x + 0.044715 * (x * x * x))))


def _sigmoid(x):
    return 1.0 / (1.0 + jnp.exp(-x))


def _mixer_in_kernel(first, x_ref, lng_ref, lnb_ref, w_in_ref, conv_w_ref, conv_b_ref, cln_g_ref,
                     cln_b_ref, pw_w_ref, pw_b_ref, sgln_g_ref, sgln_b_ref, sg_w_ref, sg_bias_ref,
                     pool_w_ref, pool_scale_ref, *rest):
    if first:
        h_ref, q_ref, k_ref, v_ref, ymix_ref, glu_buf, shift_buf, pool_buf, win_buf, conv_buf, z_buf = rest
    else:
        q_ref, k_ref, v_ref, ymix_ref, glu_buf, shift_buf, pool_buf, win_buf, conv_buf, z_buf = rest
    tm = x_ref.shape[1]
    j = pl.program_id(1)

    @pl.when(j == 0)
    def _():
        glu_buf[0:HALO, :] = jnp.zeros((HALO, D_GROUP), F32)
        pool_buf[0:HALO, :] = jnp.zeros((HALO, D_GROUP), F32)

    @pl.when(j > 0)
    def _():
        glu_buf[0:HALO, :] = glu_buf[tm:tm + HALO, :]
        pool_buf[0:HALO, :] = pool_buf[tm:tm + HALO, :]

    x = x_ref[0]
    if first:
        h = _layer_norm(x, lng_ref[...], lnb_ref[...])
        h_ref[0] = h
    else:
        h = x
    hb = h.astype(BF16)

    def proj(lo, hi):
        return jnp.dot(hb, w_in_ref[:, lo:hi], preferred_element_type=F32)

    za = proj(O_A, O_Q)
    glu_buf[HALO:HALO + tm, :] = za[:, :D_GROUP] * _sigmoid(za[:, D_GROUP:])
    q_ref[0] = (proj(O_Q, O_K) * Q_SCALE).astype(BF16)
    n_shift = HALO + tm - SUBLANES
    for s in range(1, SUBLANES):
        shift_buf[s - 1] = glu_buf[s:s + n_shift, :]

    conv_b = conv_b_ref[...]

    def conv_rows(r_lo, r_hi):
        for r0 in range(r_lo, r_hi, ROW_CHUNK):
            acc = jnp.broadcast_to(conv_b, (ROW_CHUNK, D_GROUP))
            for tap in range(CONV_WIDTH):
                off = HALO + r0 - (CONV_WIDTH - 1) + tap
                base, s = off - off % SUBLANES, off % SUBLANES
                if s == 0:
                    rows = glu_buf[base:base + ROW_CHUNK, :]
                else:
                    assert base + ROW_CHUNK <= n_shift
                    rows = shift_buf[s - 1, base:base + ROW_CHUNK, :]
                acc = acc + conv_w_ref[tap:tap + 1, :] * rows
            conv_buf[r0:r0 + ROW_CHUNK, :] = acc

    k_ref[0] = proj(O_K, O_V).astype(BF16)
    v_ref[0] = proj(O_V, O_U).astype(BF16)
    conv_rows(0, tm // 2)
    z_buf[0] = proj(O_U, O_SV)
    z_buf[1] = proj(O_SV, O_P)
    conv_rows(tm // 2, tm)
    pool_buf[HALO:HALO + tm, :] = proj(O_P, D_IN)

    hc = _layer_norm(conv_buf[...], cln_g_ref[...], cln_b_ref[...])
    hc = hc * _sigmoid(hc)
    y_conv = jnp.dot(hc.astype(BF16), pw_w_ref[...], preferred_element_type=F32) + pw_b_ref[...]
    ymix_ref[0, :, 0:D_GROUP] = y_conv.astype(BF16)

    gu = _gelu_tanh(z_buf[0])
    gv = _layer_norm(_gelu_tanh(z_buf[1]), sgln_g_ref[...], sgln_b_ref[...]).astype(BF16)
    lane = lax.broadcasted_iota(jnp.int32, (CHUNK, D_GROUP), 1)
    sg_w = sg_w_ref[...]
    sg_bias = sg_bias_ref[...]
    for c in range(tm // CHUNK):
        vc = gv[c * CHUNK:(c + 1) * CHUNK, :]
        stacked = jnp.concatenate(
            [jnp.where((lane >= hd * SG_HEAD_DIM) & (lane < (hd + 1) * SG_HEAD_DIM), vc,
                       jnp.zeros_like(vc)) for hd in range(N_SG_HEADS)], axis=0)
        mixed = jnp.dot(sg_w, stacked, preferred_element_type=F32) + sg_bias
        ymix_ref[0, c * CHUNK:(c + 1) * CHUNK, D_GROUP:2 * D_GROUP] = (
            gu[c * CHUNK:(c + 1) * CHUNK, :] * mixed).astype(BF16)

    assert POOL_WINDOWS == (2, 4, 8, 16) and HALO == 32
    n = HALO + tm
    win_buf[0, 8:n, :] = pool_buf[8:n, :] + pool_buf[7:n - 1, :]
    win_buf[1, 16:n, :] = win_buf[0, 16:n, :] + win_buf[0, 14:n - 2, :]
    win_buf[2, 24:n, :] = win_buf[1, 24:n, :] + win_buf[1, 20:n - 4, :]
    sum16 = win_buf[2, HALO:n, :] + win_buf[2, HALO - 8:n - 8, :]
    lane_r = lax.broadcasted_iota(jnp.int32, (tm, D_GROUP), 1)
    row_r = lax.broadcasted_iota(jnp.int32, (tm, D_GROUP), 0)
    group = lane_r // POOL_GROUP_DIM
    window = jnp.where(group == 0, POOL_WINDOWS[0],
                       jnp.where(group == 1, POOL_WINDOWS[1],
                                 jnp.where(group == 2, POOL_WINDOWS[2], POOL_WINDOWS[3])))
    total = jnp.where(group == 0, win_buf[0, HALO:n, :],
                      jnp.where(group == 1, win_buf[1, HALO:n, :],
                                jnp.where(group == 2, win_buf[2, HALO:n, :], sum16)))
    cnt = jnp.minimum(j * tm + row_r + 1, window).astype(F32)
    pooled = total / cnt - pool_buf[HALO:n, :]
    y_pool = jnp.dot(pooled.astype(BF16), pool_w_ref[...], preferred_element_type=F32)
    ymix_ref[0, :, 2 * D_GROUP:3 * D_GROUP] = (y_pool * pool_scale_ref[...]).astype(BF16)


def _const_spec(shape):
    return pl.BlockSpec(shape, lambda *_: (0,) * len(shape))


def _mixer_in(first, x, lng, lnb, w_in, conv_w, conv_b, cln_g, cln_b, pw_w, pw_b, sgln_g, sgln_b,
              sg_w, sg_bias, pool_w, pool_scale):
    B, S, _ = x.shape
    tm = TM_IN
    tok = lambda width: pl.BlockSpec((1, tm, width), lambda b, j: (b, j, 0))
    consts = [lng, lnb, w_in, conv_w, conv_b, cln_g, cln_b, pw_w, pw_b, sgln_g, sgln_b, sg_w,
              sg_bias, pool_w, pool_scale]
    out_shape = [jax.ShapeDtypeStruct((B, S, D_GROUP), BF16),
                 jax.ShapeDtypeStruct((B, S, D_GROUP), BF16),
                 jax.ShapeDtypeStruct((B, S, D_GROUP), BF16),
                 jax.ShapeDtypeStruct((B, S, 3 * D_GROUP), BF16)]
    out_specs = [tok(D_GROUP), tok(D_GROUP), tok(D_GROUP), tok(3 * D_GROUP)]
    if first:
        out_shape = [jax.ShapeDtypeStruct((B, S, D_MODEL), F32)] + out_shape
        out_specs = [tok(D_MODEL)] + out_specs
    return pl.pallas_call(
        functools.partial(_mixer_in_kernel, first),
        grid=(B, S // tm),
        in_specs=[tok(D_MODEL)] + [_const_spec(c.shape) for c in consts],
        out_specs=out_specs,
        out_shape=out_shape,
        scratch_shapes=[pltpu.VMEM((HALO + tm, D_GROUP), F32),
                        pltpu.VMEM((SUBLANES - 1, HALO + tm - SUBLANES, D_GROUP), F32),
                        pltpu.VMEM((HALO + tm, D_GROUP), F32),
                        pltpu.VMEM((3, HALO + tm, D_GROUP), F32),
                        pltpu.VMEM((tm, D_GROUP), F32),
                        pltpu.VMEM((2, tm, D_GROUP), F32)],
        compiler_params=pltpu.CompilerParams(dimension_semantics=("parallel", "arbitrary"),
                                             vmem_limit_bytes=VMEM_LIMIT_BYTES),
        name="mixer_in",
    )(x, *consts)


def _diff_attn_kernel(lam_init, q_ref, k_ref, v_ref, bias_ref, lq1_ref, lk1_ref, lq2_ref, lk2_ref,
                      g_ref, o_ref, vt_ref, qm_ref, s_ref, m_ref, a_ref, acc_ref):
    i = pl.program_id(1)
    n_kv = k_ref.shape[1] // TK

    @pl.when(i == 0)
    def _():
        row = lax.broadcasted_iota(jnp.int32, (V_AUG - V_DIM, TK), 0)
        ones_rows = jnp.where(row == 0, 1.0, 0.0).astype(BF16)
        for jt in range(n_kv):
            vt = v_ref[0, jt * TK:(jt + 1) * TK, :].astype(F32).T
            for hd in range(N_HEADS):
                vt_ref[jt, hd, 0:V_DIM, :] = vt[hd * V_DIM:(hd + 1) * V_DIM, :].astype(BF16)
                vt_ref[jt, hd, V_DIM:V_AUG, :] = ones_rows

    q = q_ref[0].astype(F32)
    row = lax.broadcasted_iota(jnp.int32, (LANES, TQ), 0)
    for mp in range(2):
        qt = q[:, mp * LANES:(mp + 1) * LANES].T
        for hd in range(N_HEADS):
            keep = (row >= hd * QK_DIM) & (row < (hd + 1) * QK_DIM)
            qm_ref[hd * 2 + mp] = jnp.where(keep, qt, 0.0).astype(BF16)

    m_ref[1] = jnp.full(m_ref.shape[1:], MASK_VALUE, F32)
    acc_ref[...] = jnp.zeros(acc_ref.shape, F32)

    def scores_into(slot, t, bias, hd):
        rows = pl.ds(pl.multiple_of(t * TK, TK), TK)
        for mp in range(2):
            blk = hd * 2 + mp
            cols = slice(blk * TQ, (blk + 1) * TQ)
            st = jnp.dot(k_ref[0, rows, mp * LANES:(mp + 1) * LANES], qm_ref[blk],
                         preferred_element_type=F32)
            if bias is not None:
                st = st + bias_ref[bias, :, hd * TQ:(hd + 1) * TQ]
            s_ref[slot, :, cols] = st
            m_old = m_ref[1 - slot, :, cols]
            m_new = jnp.maximum(m_old, jnp.max(st, axis=0, keepdims=True))
            m_ref[slot, :, cols] = m_new
            a_ref[slot, :, cols] = jnp.exp2(m_old - m_new)

    def values_from(slot, t, hd):
        cols = slice(hd * 2 * TQ, (hd + 1) * 2 * TQ)
        p = jnp.exp2(s_ref[slot, :, cols] - m_ref[slot, :, cols]).astype(BF16)
        pv = jnp.dot(vt_ref[t, hd], p, preferred_element_type=F32)
        acc_ref[hd] = acc_ref[hd] * a_ref[slot, :, cols] + pv

    def stage(score_args, value_args):
        for hd in range(N_HEADS):
            if score_args is not None:
                scores_into(*score_args, hd)
            if value_args is not None:
                values_from(*value_args, hd)

    stage((0, i, 0), None)

    @pl.when(i >= 1)
    def _():
        stage((1, i - 1, 1), (0, i))

    def far_pair(k2, carry):
        t = i - 2 - 2 * k2
        stage((0, t, None), (1, t + 1))
        stage((1, t - 1, None), (0, t))
        return carry

    n_far = jnp.maximum(i - 1, 0)
    lax.fori_loop(0, n_far // 2, far_pair, 0)

    @pl.when((i >= 2) & (i % 2 == 0))
    def _():
        stage((0, 0, None), (1, 1))

    @pl.when(i % 2 == 0)
    def _():
        stage(None, (0, 0))

    @pl.when(i % 2 == 1)
    def _():
        stage(None, (1, 0))

    lam = (jnp.exp(jnp.sum(lq1_ref[...] * lk1_ref[...], keepdims=True))
           - jnp.exp(jnp.sum(lq2_ref[...] * lk2_ref[...], keepdims=True)) + lam_init)
    outs = []
    for hd in range(N_HEADS):
        a = acc_ref[hd]
        o1 = a[0:V_DIM, 0:TQ] / a[V_DIM:V_DIM + 1, 0:TQ]
        o2 = a[0:V_DIM, TQ:2 * TQ] / a[V_DIM:V_DIM + 1, TQ:2 * TQ]
        o = o1 - lam * o2
        ms = jnp.mean(o * o, axis=0, keepdims=True)
        outs.append(o * lax.rsqrt(ms + LN_EPS) * g_ref[...] * (1.0 - lam_init))
    o_ref[0] = jnp.concatenate(outs, axis=0).T.astype(BF16)


def _diff_attn(lam_init, q, k, v, bias, lq1, lk1, lq2, lk2, g_col):
    B, S, _ = q.shape
    n_kv = S // TK
    return pl.pallas_call(
        functools.partial(_diff_attn_kernel, lam_init),
        grid=(B, S // TQ),
        in_specs=[pl.BlockSpec((1, TQ, D_GROUP), lambda b, i: (b, i, 0)),
                  pl.BlockSpec((1, S, D_GROUP), lambda b, i: (b, 0, 0)),
                  pl.BlockSpec((1, S, D_GROUP), lambda b, i: (b, 0, 0)),
                  _const_spec(bias.shape), _const_spec(lq1.shape), _const_spec(lk1.shape),
                  _const_spec(lq2.shape), _const_spec(lk2.shape), _const_spec(g_col.shape)],
        out_specs=pl.BlockSpec((1, TQ, D_GROUP), lambda b, i: (b, i, 0)),
        out_shape=jax.ShapeDtypeStruct((B, S, D_GROUP), BF16),
        scratch_shapes=[pltpu.VMEM((n_kv, N_HEADS, V_AUG, TK), BF16),
                        pltpu.VMEM((N_MAPS, LANES, TQ), BF16),
                        pltpu.VMEM((2, TK, N_MAPS * TQ), F32),
                        pltpu.VMEM((2, 1, N_MAPS * TQ), F32),
                        pltpu.VMEM((2, 1, N_MAPS * TQ), F32),
                        pltpu.VMEM((N_HEADS, V_AUG, 2 * TQ), F32)],
        compiler_params=pltpu.CompilerParams(dimension_semantics=("parallel", "arbitrary"),
                                             vmem_limit_bytes=VMEM_LIMIT_BYTES),
        name="diff_attn",
    )(q, k, v, bias, lq1, lk1, lq2, lk2, g_col)


def _mixer_out_kernel(h_ref, ymix_ref, ydiff_ref, w_out_ref, ln1_g_ref, ln1_b_ref, w1_ref, w2_ref,
                      ln2_g_ref, ln2_b_ref, o_ref):
    h = h_ref[...]
    ymix = ymix_ref[...]
    mix = (jnp.dot(ymix[:, 0:D_GROUP], w_out_ref[0:D_GROUP, :], preferred_element_type=F32)
           + jnp.dot(ydiff_ref[...], w_out_ref[D_GROUP:2 * D_GROUP, :], preferred_element_type=F32)
           + jnp.dot(ymix[:, D_GROUP:3 * D_GROUP], w_out_ref[2 * D_GROUP:4 * D_GROUP, :],
                     preferred_element_type=F32))
    h1 = _layer_norm(ALPHA * h + mix, ln1_g_ref[...], ln1_b_ref[...])
    a = jnp.dot(h1.astype(BF16), w1_ref[...], preferred_element_type=F32)
    a = jnp.maximum(a, 0.0)
    a = (a * a).astype(BF16)
    ff = jnp.dot(a, w2_ref[...], preferred_element_type=F32)
    o_ref[...] = _layer_norm(ALPHA * h1 + ff, ln2_g_ref[...], ln2_b_ref[...])


def _mixer_out(h, ymix, ydiff, w_out, ln1_g, ln1_b, w1, w2, ln2_g, ln2_b):
    T = h.shape[0]
    tm = TM_OUT
    tok = lambda width: pl.BlockSpec((tm, width), lambda t: (t, 0))
    consts = [w_out, ln1_g, ln1_b, w1, w2, ln2_g, ln2_b]
    single = lambda c: pl.BlockSpec(c.shape, lambda t: (0, 0), pipeline_mode=pl.Buffered(1))
    return pl.pallas_call(
        _mixer_out_kernel,
        grid=(T // tm,),
        in_specs=[tok(D_MODEL), tok(3 * D_GROUP), tok(D_GROUP)] + [single(c) for c in consts],
        out_specs=tok(D_MODEL),
        out_shape=jax.ShapeDtypeStruct((T, D_MODEL), F32),
        compiler_params=pltpu.CompilerParams(dimension_semantics=("parallel",),
                                             vmem_limit_bytes=VMEM_LIMIT_BYTES),
        name="mixer_out",
    )(h, ymix, ydiff, *consts)


def _t5_bucket(n):
    max_exact = N_BUCKETS // 2
    large = max_exact + (jnp.log(jnp.maximum(n, 1).astype(F32) / max_exact)
                         / math.log(MAX_DISTANCE / max_exact) * (N_BUCKETS - max_exact)).astype(jnp.int32)
    large = jnp.minimum(large, N_BUCKETS - 1)
    return jnp.where(n < max_exact, n, large)


def _bias_tiles(rel_bias):
    assert TK >= MAX_DISTANCE, "tiles two or more behind the diagonal must see one constant bias"
    n = 2 * (TQ + TK)
    dist = jnp.arange(TQ + TK)
    onehot = (_t5_bucket(dist)[:, None] == jnp.arange(N_BUCKETS)[None, :]).astype(F32)
    rb = rel_bias.astype(F32)
    vals = jnp.sum(onehot[:, :, None] * (rb - rb[N_BUCKETS - 1])[None, :, :], axis=1) * LOG2E
    table = jnp.concatenate([vals, jnp.full((n - TQ - TK, N_HEADS), MASK_VALUE, F32)], axis=0).T
    sheared = jnp.tile(table, (1, TK))[:, :TK * (n - 1)].reshape(N_HEADS, TK, n - 1)
    tiles = jnp.stack([sheared[:, :, 0:TQ], sheared[:, :, TK:TK + TQ]])
    return jnp.transpose(tiles, (0, 2, 1, 3)).reshape(2, TK, N_HEADS * TQ)


def _block_diag(w):
    g, c, d = w.shape
    out = jnp.zeros((g * c, g * d), w.dtype)
    for i in range(g):
        out = out.at[i * c:(i + 1) * c, i * d:(i + 1) * d].set(w[i])
    return out


def kernel(x, emb_ln_g, emb_ln_b, rel_bias, w_in, conv_w, conv_b, conv_ln_g, conv_ln_b, conv_pw_w,
           conv_pw_b, lam_q1, lam_k1, lam_q2, lam_k2, diff_norm_g, sg_ln_g, sg_ln_b, sg_w, sg_b, pool_w,
           pool_scale, w_out, ln1_g, ln1_b, w_mlp1, w_mlp2, ln2_g, ln2_b):
    B, S, _ = x.shape
    assert S % TM_IN == 0 and (B * S) % TM_OUT == 0 and S % TQ == 0 and TQ == TK
    row = lambda p: p.reshape(1, -1).astype(F32)
    bias = _bias_tiles(rel_bias)
    tril = jnp.tril(jnp.ones((CHUNK, CHUNK), F32))

    h = x
    for l in range(DEPTH):
        lam_init = 0.8 - 0.6 * math.exp(-0.3 * l)
        sg_w_cat = jnp.transpose(sg_w[l] * tril, (1, 0, 2)).reshape(CHUNK, N_SG_HEADS * CHUNK)
        sg_bias = jnp.repeat(sg_b[l].T, SG_HEAD_DIM, axis=1)
        outs = _mixer_in(
            l == 0, h, row(emb_ln_g), row(emb_ln_b), w_in[l].astype(BF16), conv_w[l].astype(F32),
            row(conv_b[l]), row(conv_ln_g[l]), row(conv_ln_b[l]), conv_pw_w[l].astype(BF16),
            row(conv_pw_b[l]), row(sg_ln_g[l]), row(sg_ln_b[l]), sg_w_cat.astype(BF16),
            sg_bias.astype(F32), _block_diag(pool_w[l]).astype(BF16), row(pool_scale[l]))
        if l == 0:
            h, q, k, v, ymix = outs
        else:
            q, k, v, ymix = outs
        ydiff = _diff_attn(lam_init, q, k, v, bias, row(lam_q1[l]), row(lam_k1[l]), row(lam_q2[l]),
                           row(lam_k2[l]), diff_norm_g[l].reshape(-1, 1).astype(F32))
        h = _mixer_out(h.reshape(B * S, D_MODEL), ymix.reshape(B * S, 3 * D_GROUP),
                       ydiff.reshape(B * S, D_GROUP), w_out[l].astype(BF16), row(ln1_g[l]),
                       row(ln1_b[l]), w_mlp1[l].astype(BF16), w_mlp2[l].astype(BF16), row(ln2_g[l]),
                       row(ln2_b[l])).reshape(B, S, D_MODEL)
    return h
```

```python
import functools
import math

import jax
import jax.numpy as jnp
import numpy as np
from jax import lax
from jax.experimental import pallas as pl
from jax.experimental.pallas import tpu as pltpu

F32 = jnp.float32
BF16 = jnp.bfloat16

D_MODEL = 1024
DEPTH = 2
D_GROUP = 256
CONV_WIDTH = 31
N_HEADS = 4
QK_DIM = 32
V_DIM = 64
N_SG_HEADS = 4
SG_HEAD_DIM = 64
CHUNK = 128
POOL_WINDOWS = (2, 4, 8, 16)
POOL_GROUP_DIM = 64
D_FF = 4096
N_BUCKETS = 32
MAX_DISTANCE = 128
LN_EPS = 1e-5
ALPHA = (2.0 * DEPTH) ** 0.25
D_IN = 2048
O_A, O_Q, O_K, O_V, O_U, O_SV, O_P = 0, 512, 768, 1024, 1280, 1536, 1792

LANES = 128
SUBLANES = 8
VMEM_LIMIT_BYTES = 56 * 1024 * 1024

TM_IN = 512
TM_OUT = 512
TQ = 256
TK = 256
HALO = 32
ROW_CHUNK = 64
V_AUG = 80
N_MAPS = 2 * N_HEADS
MASK_VALUE = -1e30
LOG2E = math.log2(math.e)
Q_SCALE = (QK_DIM ** -0.5) * LOG2E


def _layer_norm(x, g, b):
    mu = jnp.mean(x, axis=-1, keepdims=True)
    xc = x - mu
    var = jnp.mean(xc * xc, axis=-1, keepdims=True)
    return xc * lax.rsqrt(var + LN_EPS) * g + b


def _gelu_tanh(x):
    c = math.sqrt(2.0 / math.pi)
    return 0.5 * x * (1.0 + jnp.tanh(c * (x + 0.044715 * (x * x * x))))


def _sigmoid(x):
    return 1.0 / (1.0 + jnp.exp(-x))


def _mixer_in_kernel(first, x_ref, lng_ref, lnb_ref, w_in_ref, conv_w_ref, conv_b_ref, cln_g_ref,
                     cln_b_ref, pw_w_ref, pw_b_ref, sgln_g_ref, sgln_b_ref, sg_w_ref, sg_bias_ref,
                     pool_w_ref, pool_scale_ref, *rest):
    if first:
        h_ref, q_ref, k_ref, v_ref, ymix_ref, glu_buf, shift_buf, pool_buf, win_buf, conv_buf, z_buf = rest
    else:
        q_ref, k_ref, v_ref, ymix_ref, glu_buf, shift_buf, pool_buf, win_buf, conv_buf, z_buf = rest
    tm = x_ref.shape[1]
    j = pl.program_id(1)

    @pl.when(j == 0)
    def _():
        glu_buf[0:HALO, :] = jnp.zeros((HALO, D_GROUP), F32)
        pool_buf[0:HALO, :] = jnp.zeros((HALO, D_GROUP), F32)

    @pl.when(j > 0)
    def _():
        glu_buf[0:HALO, :] = glu_buf[tm:tm + HALO, :]
        pool_buf[0:HALO, :] = pool_buf[tm:tm + HALO, :]

    x = x_ref[0]
    if first:
        h = _layer_norm(x, lng_ref[...], lnb_ref[...])
        h_ref[0] = h
    else:
        h = x
    hb = h.astype(BF16)

    def proj(lo, hi):
        return jnp.dot(hb, w_in_ref[:, lo:hi], preferred_element_type=F32)

    za = proj(O_A, O_Q)
    glu_buf[HALO:HALO + tm, :] = za[:, :D_GROUP] * _sigmoid(za[:, D_GROUP:])
    q_ref[0] = (proj(O_Q, O_K) * Q_SCALE).astype(BF16)
    n_shift = HALO + tm - SUBLANES
    for s in range(1, SUBLANES):
        shift_buf[s - 1] = glu_buf[s:s + n_shift, :]

    conv_b = conv_b_ref[...]

    def conv_rows(r_lo, r_hi):
        for r0 in range(r_lo, r_hi, ROW_CHUNK):
            acc = jnp.broadcast_to(conv_b, (ROW_CHUNK, D_GROUP))
            for tap in range(CONV_WIDTH):
                off = HALO + r0 - (CONV_WIDTH - 1) + tap
                base, s = off - off % SUBLANES, off % SUBLANES
                if s == 0:
                    rows = glu_buf[base:base + ROW_CHUNK, :]
                else:
                    assert base + ROW_CHUNK <= n_shift
                    rows = shift_buf[s - 1, base:base + ROW_CHUNK, :]
                acc = acc + conv_w_ref[tap:tap + 1, :] * rows
            conv_buf[r0:r0 + ROW_CHUNK, :] = acc

    k_ref[0] = proj(O_K, O_V).astype(BF16)
    v_ref[0] = proj(O_V, O_U).astype(BF16)
    conv_rows(0, tm // 2)
    z_buf[0] = proj(O_U, O_SV)
    z_buf[1] = proj(O_SV, O_P)
    conv_rows(tm // 2, tm)
    pool_buf[HALO:HALO + tm, :] = proj(O_P, D_IN)

    hc = _layer_norm(conv_buf[...], cln_g_ref[...], cln_b_ref[...])
    hc = hc * _sigmoid(hc)
    y_conv = jnp.dot(hc.astype(BF16), pw_w_ref[...], preferred_element_type=F32) + pw_b_ref[...]
    ymix_ref[0, :, 0:D_GROUP] = y_conv.astype(BF16)

    gu = _gelu_tanh(z_buf[0])
    gv = _layer_norm(_gelu_tanh(z_buf[1]), sgln_g_ref[...], sgln_b_ref[...]).astype(BF16)
    lane = lax.broadcasted_iota(jnp.int32, (CHUNK, D_GROUP), 1)
    sg_w = sg_w_ref[...]
    sg_bias = sg_bias_ref[...]
    for c in range(tm // CHUNK):
        vc = gv[c * CHUNK:(c + 1) * CHUNK, :]
        stacked = jnp.concatenate(
            [jnp.where((lane >= hd * SG_HEAD_DIM) & (lane < (hd + 1) * SG_HEAD_DIM), vc,
                       jnp.zeros_like(vc)) for hd in range(N_SG_HEADS)], axis=0)
        mixed = jnp.dot(sg_w, stacked, preferred_element_type=F32) + sg_bias
        ymix_ref[0, c * CHUNK:(c + 1) * CHUNK, D_GROUP:2 * D_GROUP] = (
            gu[c * CHUNK:(c + 1) * CHUNK, :] * mixed).astype(BF16)

    assert POOL_WINDOWS == (2, 4, 8, 16) and HALO == 32
    n = HALO + tm
    win_buf[0, 8:n, :] = pool_buf[8:n, :] + pool_buf[7:n - 1, :]
    win_buf[1, 16:n, :] = win_buf[0, 16:n, :] + win_buf[0, 14:n - 2, :]
    win_buf[2, 24:n, :] = win_buf[1, 24:n, :] + win_buf[1, 20:n - 4, :]
    sum16 = win_buf[2, HALO:n, :] + win_buf[2, HALO - 8:n - 8, :]
    lane_r = lax.broadcasted_iota(jnp.int32, (tm, D_GROUP), 1)
    row_r = lax.broadcasted_iota(jnp.int32, (tm, D_GROUP), 0)
    group = lane_r // POOL_GROUP_DIM
    window = jnp.where(group == 0, POOL_WINDOWS[0],
                       jnp.where(group == 1, POOL_WINDOWS[1],
                                 jnp.where(group == 2, POOL_WINDOWS[2], POOL_WINDOWS[3])))
    total = jnp.where(group == 0, win_buf[0, HALO:n, :],
                      jnp.where(group == 1, win_buf[1, HALO:n, :],
                                jnp.where(group == 2, win_buf[2, HALO:n, :], sum16)))
    cnt = jnp.minimum(j * tm + row_r + 1, window).astype(F32)
    pooled = total / cnt - pool_buf[HALO:n, :]
    y_pool = jnp.dot(pooled.astype(BF16), pool_w_ref[...], preferred_element_type=F32)
    ymix_ref[0, :, 2 * D_GROUP:3 * D_GROUP] = (y_pool * pool_scale_ref[...]).astype(BF16)


def _const_spec(shape):
    return pl.BlockSpec(shape, lambda *_: (0,) * len(shape))


def _layer_spec(stacked, layer, **kwargs):
    zeros = (0,) * (stacked.ndim - 1)
    return pl.BlockSpec((None,) + stacked.shape[1:], lambda *_: (layer,) + zeros, **kwargs)


def _mixer_in(layer, x, lng, lnb, stacked):
    first = layer == 0
    B, S, _ = x.shape
    tm = TM_IN
    tok = lambda width: pl.BlockSpec((1, tm, width), lambda b, j: (b, j, 0))
    out_shape = [jax.ShapeDtypeStruct((B, S, D_GROUP), BF16),
                 jax.ShapeDtypeStruct((B, S, D_GROUP), BF16),
                 jax.ShapeDtypeStruct((B, S, D_GROUP), BF16),
                 jax.ShapeDtypeStruct((B, S, 3 * D_GROUP), BF16)]
    out_specs = [tok(D_GROUP), tok(D_GROUP), tok(D_GROUP), tok(3 * D_GROUP)]
    if first:
        out_shape = [jax.ShapeDtypeStruct((B, S, D_MODEL), F32)] + out_shape
        out_specs = [tok(D_MODEL)] + out_specs
    return pl.pallas_call(
        functools.partial(_mixer_in_kernel, first),
        grid=(B, S // tm),
        in_specs=([tok(D_MODEL), _const_spec(lng.shape), _const_spec(lnb.shape)]
                  + [_layer_spec(p, layer) for p in stacked]),
        out_specs=out_specs,
        out_shape=out_shape,
        scratch_shapes=[pltpu.VMEM((HALO + tm, D_GROUP), F32),
                        pltpu.VMEM((SUBLANES - 1, HALO + tm - SUBLANES, D_GROUP), F32),
                        pltpu.VMEM((HALO + tm, D_GROUP), F32),
                        pltpu.VMEM((3, HALO + tm, D_GROUP), F32),
                        pltpu.VMEM((tm, D_GROUP), F32),
                        pltpu.VMEM((2, tm, D_GROUP), F32)],
        compiler_params=pltpu.CompilerParams(dimension_semantics=("parallel", "arbitrary"),
                                             vmem_limit_bytes=VMEM_LIMIT_BYTES),
        name="mixer_in",
    )(x, lng, lnb, *stacked)


def _diff_attn_kernel(lam_init, q_ref, k_ref, v_ref, bias_ref, lq1_ref, lk1_ref, lq2_ref, lk2_ref,
                      g_ref, o_ref, vt_ref, qm_ref, s_ref, m_ref, a_ref, acc_ref):
    i = pl.program_id(1)
    n_kv = k_ref.shape[1] // TK

    @pl.when(i == 0)
    def _():
        row = lax.broadcasted_iota(jnp.int32, (V_AUG - V_DIM, TK), 0)
        ones_rows = jnp.where(row == 0, 1.0, 0.0).astype(BF16)
        for jt in range(n_kv):
            vt = v_ref[0, jt * TK:(jt + 1) * TK, :].astype(F32).T
            for hd in range(N_HEADS):
                vt_ref[jt, hd, 0:V_DIM, :] = vt[hd * V_DIM:(hd + 1) * V_DIM, :].astype(BF16)
                vt_ref[jt, hd, V_DIM:V_AUG, :] = ones_rows

    q = q_ref[0].astype(F32)
    row = lax.broadcasted_iota(jnp.int32, (LANES, TQ), 0)
    for mp in range(2):
        qt = q[:, mp * LANES:(mp + 1) * LANES].T
        for hd in range(N_HEADS):
            keep = (row >= hd * QK_DIM) & (row < (hd + 1) * QK_DIM)
            qm_ref[hd * 2 + mp] = jnp.where(keep, qt, 0.0).astype(BF16)

    m_ref[1] = jnp.full(m_ref.shape[1:], MASK_VALUE, F32)
    acc_ref[...] = jnp.zeros(acc_ref.shape, F32)

    def scores_into(slot, t, bias, hd):
        rows = pl.ds(pl.multiple_of(t * TK, TK), TK)
        for mp in range(2):
            blk = hd * 2 + mp
            cols = slice(blk * TQ, (blk + 1) * TQ)
            st = jnp.dot(k_ref[0, rows, mp * LANES:(mp + 1) * LANES], qm_ref[blk],
                         preferred_element_type=F32)
            if bias is not None:
                st = st + bias_ref[bias, :, hd * TQ:(hd + 1) * TQ]
            s_ref[slot, :, cols] = st
            m_old = m_ref[1 - slot, :, cols]
            m_new = jnp.maximum(m_old, jnp.max(st, axis=0, keepdims=True))
            m_ref[slot, :, cols] = m_new
            a_ref[slot, :, cols] = jnp.exp2(m_old - m_new)

    def values_from(slot, t, hd):
        cols = slice(hd * 2 * TQ, (hd + 1) * 2 * TQ)
        p = jnp.exp2(s_ref[slot, :, cols] - m_ref[slot, :, cols]).astype(BF16)
        pv = jnp.dot(vt_ref[t, hd], p, preferred_element_type=F32)
        acc_ref[hd] = acc_ref[hd] * a_ref[slot, :, cols] + pv

    def stage(score_args, value_args):
        for hd in range(N_HEADS):
            if score_args is not None:
                scores_into(*score_args, hd)
            if value_args is not None:
                values_from(*value_args, hd)

    stage((0, i, 0), None)

    @pl.when(i >= 1)
    def _():
        stage((1, i - 1, 1), (0, i))

    def far_pair(k2, carry):
        t = i - 2 - 2 * k2
        stage((0, t, None), (1, t + 1))
        stage((1, t - 1, None), (0, t))
        return carry

    n_far = jnp.maximum(i - 1, 0)
    lax.fori_loop(0, n_far // 2, far_pair, 0)

    @pl.when((i >= 2) & (i % 2 == 0))
    def _():
        stage((0, 0, None), (1, 1))

    @pl.when(i % 2 == 0)
    def _():
        stage(None, (0, 0))

    @pl.when(i % 2 == 1)
    def _():
        stage(None, (1, 0))

    lam = (jnp.exp(jnp.sum(lq1_ref[...] * lk1_ref[...], keepdims=True))
           - jnp.exp(jnp.sum(lq2_ref[...] * lk2_ref[...], keepdims=True)) + lam_init)
    outs = []
    for hd in range(N_HEADS):
        a = acc_ref[hd]
        o1 = a[0:V_DIM, 0:TQ] / a[V_DIM:V_DIM + 1, 0:TQ]
        o2 = a[0:V_DIM, TQ:2 * TQ] / a[V_DIM:V_DIM + 1, TQ:2 * TQ]
        o = o1 - lam * o2
        ms = jnp.mean(o * o, axis=0, keepdims=True)
        outs.append(o * lax.rsqrt(ms + LN_EPS) * g_ref[...] * (1.0 - lam_init))
    o_ref[0] = jnp.concatenate(outs, axis=0).T.astype(BF16)


def _diff_attn(layer, q, k, v, bias, stacked):
    lam_init = 0.8 - 0.6 * math.exp(-0.3 * layer)
    B, S, _ = q.shape
    n_kv = S // TK
    return pl.pallas_call(
        functools.partial(_diff_attn_kernel, lam_init),
        grid=(B, S // TQ),
        in_specs=[pl.BlockSpec((1, TQ, D_GROUP), lambda b, i: (b, i, 0)),
                  pl.BlockSpec((1, S, D_GROUP), lambda b, i: (b, 0, 0)),
                  pl.BlockSpec((1, S, D_GROUP), lambda b, i: (b, 0, 0)),
                  _const_spec(bias.shape)] + [_layer_spec(p, layer) for p in stacked],
        out_specs=pl.BlockSpec((1, TQ, D_GROUP), lambda b, i: (b, i, 0)),
        out_shape=jax.ShapeDtypeStruct((B, S, D_GROUP), BF16),
        scratch_shapes=[pltpu.VMEM((n_kv, N_HEADS, V_AUG, TK), BF16),
                        pltpu.VMEM((N_MAPS, LANES, TQ), BF16),
                        pltpu.VMEM((2, TK, N_MAPS * TQ), F32),
                        pltpu.VMEM((2, 1, N_MAPS * TQ), F32),
                        pltpu.VMEM((2, 1, N_MAPS * TQ), F32),
                        pltpu.VMEM((N_HEADS, V_AUG, 2 * TQ), F32)],
        compiler_params=pltpu.CompilerParams(dimension_semantics=("parallel", "arbitrary"),
                                             vmem_limit_bytes=VMEM_LIMIT_BYTES),
        name="diff_attn",
    )(q, k, v, bias, *stacked)


def _mixer_out_kernel(h_ref, ymix_ref, ydiff_ref, w_out_ref, ln1_g_ref, ln1_b_ref, w1_ref, w2_ref,
                      ln2_g_ref, ln2_b_ref, o_ref):
    h = h_ref[...]
    ymix = ymix_ref[...]
    mix = (jnp.dot(ymix[:, 0:D_GROUP], w_out_ref[0:D_GROUP, :], preferred_element_type=F32)
           + jnp.dot(ydiff_ref[...], w_out_ref[D_GROUP:2 * D_GROUP, :], preferred_element_type=F32)
           + jnp.dot(ymix[:, D_GROUP:3 * D_GROUP], w_out_ref[2 * D_GROUP:4 * D_GROUP, :],
                     preferred_element_type=F32))
    h1 = _layer_norm(ALPHA * h + mix, ln1_g_ref[...], ln1_b_ref[...])
    a = jnp.dot(h1.astype(BF16), w1_ref[...], preferred_element_type=F32)
    a = jnp.maximum(a, 0.0)
    a = (a * a).astype(BF16)
    ff = jnp.dot(a, w2_ref[...], preferred_element_type=F32)
    o_ref[...] = _layer_norm(ALPHA * h1 + ff, ln2_g_ref[...], ln2_b_ref[...])


def _mixer_out(layer, h, ymix, ydiff, stacked):
    T = h.shape[0]
    tm = TM_OUT
    tok = lambda width: pl.BlockSpec((tm, width), lambda t: (t, 0))
    return pl.pallas_call(
        _mixer_out_kernel,
        grid=(T // tm,),
        in_specs=([tok(D_MODEL), tok(3 * D_GROUP), tok(D_GROUP)]
                  + [_layer_spec(p, layer, pipeline_mode=pl.Buffered(1)) for p in stacked]),
        out_specs=tok(D_MODEL),
        out_shape=jax.ShapeDtypeStruct((T, D_MODEL), F32),
        compiler_params=pltpu.CompilerParams(dimension_semantics=("parallel",),
                                             vmem_limit_bytes=VMEM_LIMIT_BYTES),
        name="mixer_out",
    )(h, ymix, ydiff, *stacked)


def _t5_bucket(n):
    max_exact = N_BUCKETS // 2
    large = max_exact + (jnp.log(jnp.maximum(n, 1).astype(F32) / max_exact)
                         / math.log(MAX_DISTANCE / max_exact) * (N_BUCKETS - max_exact)).astype(jnp.int32)
    large = jnp.minimum(large, N_BUCKETS - 1)
    return jnp.where(n < max_exact, n, large)


def _bias_tiles(rel_bias):
    assert TK >= MAX_DISTANCE, "tiles two or more behind the diagonal must see one constant bias"
    n = 2 * (TQ + TK)
    dist = jnp.arange(TQ + TK)
    onehot = (_t5_bucket(dist)[:, None] == jnp.arange(N_BUCKETS)[None, :]).astype(F32)
    rb = rel_bias.astype(F32)
    vals = jnp.sum(onehot[:, :, None] * (rb - rb[N_BUCKETS - 1])[None, :, :], axis=1) * LOG2E
    table = jnp.concatenate([vals, jnp.full((n - TQ - TK, N_HEADS), MASK_VALUE, F32)], axis=0).T
    sheared = jnp.tile(table, (1, TK))[:, :TK * (n - 1)].reshape(N_HEADS, TK, n - 1)
    tiles = jnp.stack([sheared[:, :, 0:TQ], sheared[:, :, TK:TK + TQ]])
    return jnp.transpose(tiles, (0, 2, 1, 3)).reshape(2, TK, N_HEADS * TQ)


def _block_diag(w):
    n_layers, g, c, d = w.shape
    out = jnp.zeros((n_layers, g * c, g * d), w.dtype)
    for i in range(g):
        out = out.at[:, i * c:(i + 1) * c, i * d:(i + 1) * d].set(w[:, i])
    return out


def kernel(x, emb_ln_g, emb_ln_b, rel_bias, w_in, conv_w, conv_b, conv_ln_g, conv_ln_b, conv_pw_w,
           conv_pw_b, lam_q1, lam_k1, lam_q2, lam_k2, diff_norm_g, sg_ln_g, sg_ln_b, sg_w, sg_b, pool_w,
           pool_scale, w_out, ln1_g, ln1_b, w_mlp1, w_mlp2, ln2_g, ln2_b):
    B, S, _ = x.shape
    assert S % TM_IN == 0 and (B * S) % TM_OUT == 0 and S % TQ == 0 and TQ == TK
    rows = lambda p: p.reshape(DEPTH, 1, -1).astype(F32)
    tril = jnp.tril(jnp.ones((CHUNK, CHUNK), F32))
    sg_w_cat = jnp.transpose(sg_w * tril, (0, 2, 1, 3)).reshape(DEPTH, CHUNK, N_SG_HEADS * CHUNK)
    sg_bias = jnp.repeat(jnp.swapaxes(sg_b, 1, 2), SG_HEAD_DIM, axis=2).astype(F32)
    in_params = [w_in.astype(BF16), conv_w.astype(F32), rows(conv_b), rows(conv_ln_g), rows(conv_ln_b),
                 conv_pw_w.astype(BF16), rows(conv_pw_b), rows(sg_ln_g), rows(sg_ln_b),
                 sg_w_cat.astype(BF16), sg_bias, _block_diag(pool_w).astype(BF16), rows(pool_scale)]
    attn_params = [rows(lam_q1), rows(lam_k1), rows(lam_q2), rows(lam_k2),
                   diff_norm_g.reshape(DEPTH, -1, 1).astype(F32)]
    out_params = [w_out.astype(BF16), rows(ln1_g), rows(ln1_b), w_mlp1.astype(BF16),
                  w_mlp2.astype(BF16), rows(ln2_g), rows(ln2_b)]
    bias = _bias_tiles(rel_bias)
    emb_g, emb_b = emb_ln_g.reshape(1, -1).astype(F32), emb_ln_b.reshape(1, -1).astype(F32)

    h = x
    for l in range(DEPTH):
        outs = _mixer_in(l, h, emb_g, emb_b, in_params)
        if l == 0:
            h, q, k, v, ymix = outs
        else:
            q, k, v, ymix = outs
        ydiff = _diff_attn(l, q, k, v, bias, attn_params)
        h = _mixer_out(l, h.reshape(B * S, D_MODEL), ymix.reshape(B * S, 3 * D_GROUP),
                       ydiff.reshape(B * S, D_GROUP), out_params).reshape(B, S, D_MODEL)
    return h
```

```python
import functools
import math

import jax
import jax.numpy as jnp
import numpy as np
from jax import lax
from jax.experimental import pallas as pl
from jax.experimental.pallas import tpu as pltpu

F32 = jnp.float32
BF16 = jnp.bfloat16

D_MODEL = 1024
DEPTH = 2
D_GROUP = 256
CONV_WIDTH = 31
N_HEADS = 4
QK_DIM = 32
V_DIM = 64
N_SG_HEADS = 4
SG_HEAD_DIM = 64
CHUNK = 128
POOL_WINDOWS = (2, 4, 8, 16)
POOL_GROUP_DIM = 64
D_FF = 4096
N_BUCKETS = 32
MAX_DISTANCE = 128
LN_EPS = 1e-5
ALPHA = (2.0 * DEPTH) ** 0.25
D_IN = 2048
O_A, O_Q, O_K, O_V, O_U, O_SV, O_P = 0, 512, 768, 1024, 1280, 1536, 1792

LANES = 128
SUBLANES = 8
VMEM_LIMIT_BYTES = 56 * 1024 * 1024

TM_IN = 512
TM_OUT = 512
TQ = 256
TK = 256
HALO = 32
ROW_CHUNK = 64
V_AUG = 80
N_MAPS = 2 * N_HEADS
MASK_VALUE = -1e30
LOG2E = math.log2(math.e)
Q_SCALE = (QK_DIM ** -0.5) * LOG2E


def _layer_norm(x, g, b):
    mu = jnp.mean(x, axis=-1, keepdims=True)
    xc = x - mu
    var = jnp.mean(xc * xc, axis=-1, keepdims=True)
    return xc * lax.rsqrt(var + LN_EPS) * g + b


def _gelu_tanh(x):
    c = math.sqrt(2.0 / math.pi)
    return 0.5 * x * (1.0 + jnp.tanh(c * (x + 0.044715 * (x * x * x))))


def _sigmoid(x):
    return 1.0 / (1.0 + jnp.exp(-x))


def _mixer_in_kernel(first, x_ref, emb_ln_ref, w_in_ref, conv_w_ref, pw_w_ref, sg_w_ref, sg_bias_ref,
                     pool_w_ref, vec_ref, *rest):
    conv_b, cln_g, cln_b, pw_b, sgln_g, sgln_b, pool_scale = (vec_ref[r:r + 1, :] for r in range(7))
    if first:
        h_ref, q_ref, k_ref, v_ref, ymix_ref, glu_buf, shift_buf, pool_buf, win_buf, conv_buf, z_buf = rest
    else:
        q_ref, k_ref, v_ref, ymix_ref, glu_buf, shift_buf, pool_buf, win_buf, conv_buf, z_buf = rest
    tm = x_ref.shape[1]
    j = pl.program_id(1)

    @pl.when(j == 0)
    def _():
        glu_buf[0:HALO, :] = jnp.zeros((HALO, D_GROUP), F32)
        pool_buf[0:HALO, :] = jnp.zeros((HALO, D_GROUP), F32)

    @pl.when(j > 0)
    def _():
        glu_buf[0:HALO, :] = glu_buf[tm:tm + HALO, :]
        pool_buf[0:HALO, :] = pool_buf[tm:tm + HALO, :]

    x = x_ref[0]
    if first:
        h = _layer_norm(x, emb_ln_ref[0:1, :], emb_ln_ref[1:2, :])
        h_ref[0] = h
    else:
        h = x
    hb = h.astype(BF16)

    def proj(lo, hi):
        return jnp.dot(hb, w_in_ref[:, lo:hi], preferred_element_type=F32)

    za = proj(O_A, O_Q)
    glu_buf[HALO:HALO + tm, :] = za[:, :D_GROUP] * _sigmoid(za[:, D_GROUP:])
    q_ref[0] = (proj(O_Q, O_K) * Q_SCALE).astype(BF16)
    n_shift = HALO + tm - SUBLANES
    for s in range(1, SUBLANES):
        shift_buf[s - 1] = glu_buf[s:s + n_shift, :]

    def conv_rows(r_lo, r_hi):
        for r0 in range(r_lo, r_hi, ROW_CHUNK):
            acc = jnp.broadcast_to(conv_b, (ROW_CHUNK, D_GROUP))
            for tap in range(CONV_WIDTH):
                off = HALO + r0 - (CONV_WIDTH - 1) + tap
                base, s = off - off % SUBLANES, off % SUBLANES
                if s == 0:
                    rows = glu_buf[base:base + ROW_CHUNK, :]
                else:
                    assert base + ROW_CHUNK <= n_shift
                    rows = shift_buf[s - 1, base:base + ROW_CHUNK, :]
                acc = acc + conv_w_ref[tap:tap + 1, :] * rows
            conv_buf[r0:r0 + ROW_CHUNK, :] = acc

    k_ref[0] = proj(O_K, O_V).astype(BF16)
    v_ref[0] = proj(O_V, O_U).astype(BF16)
    conv_rows(0, tm // 2)
    z_buf[0] = proj(O_U, O_SV)
    z_buf[1] = proj(O_SV, O_P)
    conv_rows(tm // 2, tm)
    pool_buf[HALO:HALO + tm, :] = proj(O_P, D_IN)

    hc = _layer_norm(conv_buf[...], cln_g, cln_b)
    hc = hc * _sigmoid(hc)
    y_conv = jnp.dot(hc.astype(BF16), pw_w_ref[...], preferred_element_type=F32) + pw_b
    ymix_ref[0, :, 0:D_GROUP] = y_conv.astype(BF16)

    gu = _gelu_tanh(z_buf[0])
    gv = _layer_norm(_gelu_tanh(z_buf[1]), sgln_g, sgln_b).astype(BF16)
    lane = lax.broadcasted_iota(jnp.int32, (CHUNK, D_GROUP), 1)
    sg_w = sg_w_ref[...]
    sg_bias = sg_bias_ref[...]
    for c in range(tm // CHUNK):
        vc = gv[c * CHUNK:(c + 1) * CHUNK, :]
        stacked = jnp.concatenate(
            [jnp.where((lane >= hd * SG_HEAD_DIM) & (lane < (hd + 1) * SG_HEAD_DIM), vc,
                       jnp.zeros_like(vc)) for hd in range(N_SG_HEADS)], axis=0)
        mixed = jnp.dot(sg_w, stacked, preferred_element_type=F32) + sg_bias
        ymix_ref[0, c * CHUNK:(c + 1) * CHUNK, D_GROUP:2 * D_GROUP] = (
            gu[c * CHUNK:(c + 1) * CHUNK, :] * mixed).astype(BF16)

    assert POOL_WINDOWS == (2, 4, 8, 16) and HALO == 32
    n = HALO + tm
    win_buf[0, 8:n, :] = pool_buf[8:n, :] + pool_buf[7:n - 1, :]
    win_buf[1, 16:n, :] = win_buf[0, 16:n, :] + win_buf[0, 14:n - 2, :]
    win_buf[2, 24:n, :] = win_buf[1, 24:n, :] + win_buf[1, 20:n - 4, :]
    sum16 = win_buf[2, HALO:n, :] + win_buf[2, HALO - 8:n - 8, :]
    lane_r = lax.broadcasted_iota(jnp.int32, (tm, D_GROUP), 1)
    row_r = lax.broadcasted_iota(jnp.int32, (tm, D_GROUP), 0)
    group = lane_r // POOL_GROUP_DIM
    window = jnp.where(group == 0, POOL_WINDOWS[0],
                       jnp.where(group == 1, POOL_WINDOWS[1],
                                 jnp.where(group == 2, POOL_WINDOWS[2], POOL_WINDOWS[3])))
    total = jnp.where(group == 0, win_buf[0, HALO:n, :],
                      jnp.where(group == 1, win_buf[1, HALO:n, :],
                                jnp.where(group == 2, win_buf[2, HALO:n, :], sum16)))
    cnt = jnp.minimum(j * tm + row_r + 1, window).astype(F32)
    pooled = total / cnt - pool_buf[HALO:n, :]
    y_pool = jnp.dot(pooled.astype(BF16), pool_w_ref[...], preferred_element_type=F32)
    ymix_ref[0, :, 2 * D_GROUP:3 * D_GROUP] = (y_pool * pool_scale).astype(BF16)


def _const_spec(shape):
    return pl.BlockSpec(shape, lambda *_: (0,) * len(shape))


def _layer_spec(stacked, layer, **kwargs):
    zeros = (0,) * (stacked.ndim - 1)
    return pl.BlockSpec((None,) + stacked.shape[1:], lambda *_: (layer,) + zeros, **kwargs)


def _mixer_in(layer, x, emb_ln, stacked):
    first = layer == 0
    B, S, _ = x.shape
    tm = TM_IN
    tok = lambda width: pl.BlockSpec((1, tm, width), lambda b, j: (b, j, 0))
    out_shape = [jax.ShapeDtypeStruct((B, S, D_GROUP), BF16),
                 jax.ShapeDtypeStruct((B, S, D_GROUP), BF16),
                 jax.ShapeDtypeStruct((B, S, D_GROUP), BF16),
                 jax.ShapeDtypeStruct((B, S, 3 * D_GROUP), BF16)]
    out_specs = [tok(D_GROUP), tok(D_GROUP), tok(D_GROUP), tok(3 * D_GROUP)]
    if first:
        out_shape = [jax.ShapeDtypeStruct((B, S, D_MODEL), F32)] + out_shape
        out_specs = [tok(D_MODEL)] + out_specs
    return pl.pallas_call(
        functools.partial(_mixer_in_kernel, first),
        grid=(B, S // tm),
        in_specs=[tok(D_MODEL), _const_spec(emb_ln.shape)] + [_layer_spec(p, layer) for p in stacked],
        out_specs=out_specs,
        out_shape=out_shape,
        scratch_shapes=[pltpu.VMEM((HALO + tm, D_GROUP), F32),
                        pltpu.VMEM((SUBLANES - 1, HALO + tm - SUBLANES, D_GROUP), F32),
                        pltpu.VMEM((HALO + tm, D_GROUP), F32),
                        pltpu.VMEM((3, HALO + tm, D_GROUP), F32),
                        pltpu.VMEM((tm, D_GROUP), F32),
                        pltpu.VMEM((2, tm, D_GROUP), F32)],
        compiler_params=pltpu.CompilerParams(dimension_semantics=("parallel", "arbitrary"),
                                             vmem_limit_bytes=VMEM_LIMIT_BYTES),
        name="mixer_in",
    )(x, emb_ln, *stacked)


def _diff_attn_kernel(lam_init, q_ref, k_ref, v_ref, bias_ref, lam_ref, g_ref, o_ref, vt_ref, qm_ref,
                      s_ref, m_ref, a_ref, acc_ref):
    i = pl.program_id(1)
    n_kv = k_ref.shape[1] // TK

    @pl.when(i == 0)
    def _():
        row = lax.broadcasted_iota(jnp.int32, (V_AUG - V_DIM, TK), 0)
        ones_rows = jnp.where(row == 0, 1.0, 0.0).astype(BF16)
        for jt in range(n_kv):
            vt = v_ref[0, jt * TK:(jt + 1) * TK, :].astype(F32).T
            for hd in range(N_HEADS):
                vt_ref[jt, hd, 0:V_DIM, :] = vt[hd * V_DIM:(hd + 1) * V_DIM, :].astype(BF16)
                vt_ref[jt, hd, V_DIM:V_AUG, :] = ones_rows

    q = q_ref[0].astype(F32)
    row = lax.broadcasted_iota(jnp.int32, (LANES, TQ), 0)
    for mp in range(2):
        qt = q[:, mp * LANES:(mp + 1) * LANES].T
        for hd in range(N_HEADS):
            keep = (row >= hd * QK_DIM) & (row < (hd + 1) * QK_DIM)
            qm_ref[hd * 2 + mp] = jnp.where(keep, qt, 0.0).astype(BF16)

    m_ref[1] = jnp.full(m_ref.shape[1:], MASK_VALUE, F32)
    acc_ref[...] = jnp.zeros(acc_ref.shape, F32)

    def scores_into(slot, t, bias, hd):
        rows = pl.ds(pl.multiple_of(t * TK, TK), TK)
        for mp in range(2):
            blk = hd * 2 + mp
            cols = slice(blk * TQ, (blk + 1) * TQ)
            st = jnp.dot(k_ref[0, rows, mp * LANES:(mp + 1) * LANES], qm_ref[blk],
                         preferred_element_type=F32)
            if bias is not None:
                st = st + bias_ref[bias, :, hd * TQ:(hd + 1) * TQ]
            s_ref[slot, :, cols] = st
            m_old = m_ref[1 - slot, :, cols]
            m_new = jnp.maximum(m_old, jnp.max(st, axis=0, keepdims=True))
            m_ref[slot, :, cols] = m_new
            a_ref[slot, :, cols] = jnp.exp2(m_old - m_new)

    def values_from(slot, t, hd):
        cols = slice(hd * 2 * TQ, (hd + 1) * 2 * TQ)
        p = jnp.exp2(s_ref[slot, :, cols] - m_ref[slot, :, cols]).astype(BF16)
        pv = jnp.dot(vt_ref[t, hd], p, preferred_element_type=F32)
        acc_ref[hd] = acc_ref[hd] * a_ref[slot, :, cols] + pv

    def stage(score_args, value_args):
        for hd in range(N_HEADS):
            if score_args is not None:
                scores_into(*score_args, hd)
            if value_args is not None:
                values_from(*value_args, hd)

    stage((0, i, 0), None)

    @pl.when(i >= 1)
    def _():
        stage((1, i - 1, 1), (0, i))

    def far_pair(k2, carry):
        t = i - 2 - 2 * k2
        stage((0, t, None), (1, t + 1))
        stage((1, t - 1, None), (0, t))
        return carry

    n_far = jnp.maximum(i - 1, 0)
    lax.fori_loop(0, n_far // 2, far_pair, 0)

    @pl.when((i >= 2) & (i % 2 == 0))
    def _():
        stage((0, 0, None), (1, 1))

    def last_values_and_finalize(slot):
        lam = (jnp.exp(jnp.sum(lam_ref[0:1, :] * lam_ref[1:2, :], keepdims=True))
               - jnp.exp(jnp.sum(lam_ref[2:3, :] * lam_ref[3:4, :], keepdims=True)) + lam_init)
        outs = []
        for hd in range(N_HEADS):
            values_from(slot, 0, hd)
            a = acc_ref[hd]
            o1 = a[0:V_DIM, 0:TQ] / a[V_DIM:V_DIM + 1, 0:TQ]
            o2 = a[0:V_DIM, TQ:2 * TQ] / a[V_DIM:V_DIM + 1, TQ:2 * TQ]
            o = o1 - lam * o2
            ms = jnp.mean(o * o, axis=0, keepdims=True)
            outs.append(o * lax.rsqrt(ms + LN_EPS) * g_ref[...] * (1.0 - lam_init))
        o_ref[0] = jnp.concatenate(outs, axis=0).T.astype(BF16)

    @pl.when(i % 2 == 0)
    def _():
        last_values_and_finalize(0)

    @pl.when(i % 2 == 1)
    def _():
        last_values_and_finalize(1)


def _diff_attn(layer, q, k, v, bias, stacked):
    lam_init = 0.8 - 0.6 * math.exp(-0.3 * layer)
    B, S, _ = q.shape
    n_kv = S // TK
    return pl.pallas_call(
        functools.partial(_diff_attn_kernel, lam_init),
        grid=(B, S // TQ),
        in_specs=[pl.BlockSpec((1, TQ, D_GROUP), lambda b, i: (b, i, 0)),
                  pl.BlockSpec((1, S, D_GROUP), lambda b, i: (b, 0, 0)),
                  pl.BlockSpec((1, S, D_GROUP), lambda b, i: (b, 0, 0)),
                  _const_spec(bias.shape)] + [_layer_spec(p, layer) for p in stacked],
        out_specs=pl.BlockSpec((1, TQ, D_GROUP), lambda b, i: (b, i, 0)),
        out_shape=jax.ShapeDtypeStruct((B, S, D_GROUP), BF16),
        scratch_shapes=[pltpu.VMEM((n_kv, N_HEADS, V_AUG, TK), BF16),
                        pltpu.VMEM((N_MAPS, LANES, TQ), BF16),
                        pltpu.VMEM((2, TK, N_MAPS * TQ), F32),
                        pltpu.VMEM((2, 1, N_MAPS * TQ), F32),
                        pltpu.VMEM((2, 1, N_MAPS * TQ), F32),
                        pltpu.VMEM((N_HEADS, V_AUG, 2 * TQ), F32)],
        compiler_params=pltpu.CompilerParams(dimension_semantics=("parallel", "arbitrary"),
                                             vmem_limit_bytes=VMEM_LIMIT_BYTES),
        name="diff_attn",
    )(q, k, v, bias, *stacked)


def _mixer_out_kernel(h_ref, ymix_ref, ydiff_ref, w_out_ref, w1_ref, w2_ref, ln_ref, o_ref):
    ln1_g, ln1_b, ln2_g, ln2_b = (ln_ref[r:r + 1, :] for r in range(4))
    h = h_ref[...]
    ymix = ymix_ref[...]
    mix = (jnp.dot(ymix[:, 0:D_GROUP], w_out_ref[0:D_GROUP, :], preferred_element_type=F32)
           + jnp.dot(ydiff_ref[...], w_out_ref[D_GROUP:2 * D_GROUP, :], preferred_element_type=F32)
           + jnp.dot(ymix[:, D_GROUP:3 * D_GROUP], w_out_ref[2 * D_GROUP:4 * D_GROUP, :],
                     preferred_element_type=F32))
    h1 = _layer_norm(ALPHA * h + mix, ln1_g, ln1_b)
    a = jnp.dot(h1.astype(BF16), w1_ref[...], preferred_element_type=F32)
    a = jnp.maximum(a, 0.0)
    a = (a * a).astype(BF16)
    ff = jnp.dot(a, w2_ref[...], preferred_element_type=F32)
    o_ref[...] = _layer_norm(ALPHA * h1 + ff, ln2_g, ln2_b)


def _mixer_out(layer, h, ymix, ydiff, stacked):
    T = h.shape[0]
    tm = TM_OUT
    tok = lambda width: pl.BlockSpec((tm, width), lambda t: (t, 0))
    return pl.pallas_call(
        _mixer_out_kernel,
        grid=(T // tm,),
        in_specs=([tok(D_MODEL), tok(3 * D_GROUP), tok(D_GROUP)]
                  + [_layer_spec(p, layer, pipeline_mode=pl.Buffered(1)) for p in stacked]),
        out_specs=tok(D_MODEL),
        out_shape=jax.ShapeDtypeStruct((T, D_MODEL), F32),
        compiler_params=pltpu.CompilerParams(dimension_semantics=("parallel",),
                                             vmem_limit_bytes=VMEM_LIMIT_BYTES),
        name="mixer_out",
    )(h, ymix, ydiff, *stacked)


def _t5_bucket(n):
    max_exact = N_BUCKETS // 2
    large = max_exact + (jnp.log(jnp.maximum(n, 1).astype(F32) / max_exact)
                         / math.log(MAX_DISTANCE / max_exact) * (N_BUCKETS - max_exact)).astype(jnp.int32)
    large = jnp.minimum(large, N_BUCKETS - 1)
    return jnp.where(n < max_exact, n, large)


def _bias_tiles(rel_bias):
    assert TK >= MAX_DISTANCE, "tiles two or more behind the diagonal must see one constant bias"
    n = 2 * (TQ + TK)
    dist = jnp.arange(TQ + TK)
    onehot = (_t5_bucket(dist)[:, None] == jnp.arange(N_BUCKETS)[None, :]).astype(F32)
    rb = rel_bias.astype(F32)
    vals = jnp.sum(onehot[:, :, None] * (rb - rb[N_BUCKETS - 1])[None, :, :], axis=1) * LOG2E
    table = jnp.concatenate([vals, jnp.full((n - TQ - TK, N_HEADS), MASK_VALUE, F32)], axis=0).T
    sheared = jnp.tile(table, (1, TK))[:, :TK * (n - 1)].reshape(N_HEADS, TK, n - 1)
    tiles = jnp.stack([sheared[:, :, 0:TQ], sheared[:, :, TK:TK + TQ]])
    return jnp.transpose(tiles, (0, 2, 1, 3)).reshape(2, TK, N_HEADS * TQ)


def _block_diag(w):
    n_layers, g, c, d = w.shape
    out = jnp.zeros((n_layers, g * c, g * d), w.dtype)
    for i in range(g):
        out = out.at[:, i * c:(i + 1) * c, i * d:(i + 1) * d].set(w[:, i])
    return out


def kernel(x, emb_ln_g, emb_ln_b, rel_bias, w_in, conv_w, conv_b, conv_ln_g, conv_ln_b, conv_pw_w,
           conv_pw_b, lam_q1, lam_k1, lam_q2, lam_k2, diff_norm_g, sg_ln_g, sg_ln_b, sg_w, sg_b, pool_w,
           pool_scale, w_out, ln1_g, ln1_b, w_mlp1, w_mlp2, ln2_g, ln2_b):
    B, S, _ = x.shape
    assert S % TM_IN == 0 and (B * S) % TM_OUT == 0 and S % TQ == 0 and TQ == TK
    pack = lambda *ps: jnp.stack([p.astype(F32) for p in ps], axis=1)
    tril = jnp.tril(jnp.ones((CHUNK, CHUNK), F32))
    sg_w_cat = jnp.transpose(sg_w * tril, (0, 2, 1, 3)).reshape(DEPTH, CHUNK, N_SG_HEADS * CHUNK)
    sg_bias = jnp.repeat(jnp.swapaxes(sg_b, 1, 2), SG_HEAD_DIM, axis=2).astype(F32)
    in_params = [w_in.astype(BF16), conv_w.astype(F32), conv_pw_w.astype(BF16), sg_w_cat.astype(BF16),
                 sg_bias, _block_diag(pool_w).astype(BF16),
                 pack(conv_b, conv_ln_g, conv_ln_b, conv_pw_b, sg_ln_g, sg_ln_b, pool_scale,
                      jnp.zeros_like(pool_scale))]
    attn_params = [pack(lam_q1, lam_k1, lam_q2, lam_k2), diff_norm_g.reshape(DEPTH, -1, 1).astype(F32)]
    out_params = [w_out.astype(BF16), w_mlp1.astype(BF16), w_mlp2.astype(BF16),
                  pack(ln1_g, ln1_b, ln2_g, ln2_b)]
    bias = _bias_tiles(rel_bias)
    emb_ln = jnp.stack([emb_ln_g, emb_ln_b]).astype(F32)

    h = x
    for l in range(DEPTH):
        outs = _mixer_in(l, h, emb_ln, in_params)
        if l == 0:
            h, q, k, v, ymix = outs
        else:
            q, k, v, ymix = outs
        ydiff = _diff_attn(l, q, k, v, bias, attn_params)
        h = _mixer_out(l, h.reshape(B * S, D_MODEL), ymix.reshape(B * S, 3 * D_GROUP),
                       ydiff.reshape(B * S, D_GROUP), out_params).reshape(B, S, D_MODEL)
    return h
```

```python
import functools
import math

import jax
import jax.numpy as jnp
import numpy as np
from jax import lax
from jax.experimental import pallas as pl
from jax.experimental.pallas import tpu as pltpu

F32 = jnp.float32
BF16 = jnp.bfloat16

D_MODEL = 1024
DEPTH = 2
D_GROUP = 256
CONV_WIDTH = 31
N_HEADS = 4
QK_DIM = 32
V_DIM = 64
N_SG_HEADS = 4
SG_HEAD_DIM = 64
CHUNK = 128
POOL_WINDOWS = (2, 4, 8, 16)
POOL_GROUP_DIM = 64
D_FF = 4096
N_BUCKETS = 32
MAX_DISTANCE = 128
LN_EPS = 1e-5
ALPHA = (2.0 * DEPTH) ** 0.25
D_IN = 2048
O_A, O_Q, O_K, O_V, O_U, O_SV, O_P = 0, 512, 768, 1024, 1280, 1536, 1792

LANES = 128
SUBLANES = 8
VMEM_LIMIT_BYTES = 56 * 1024 * 1024

TM_IN = 512
TM_OUT = 512
TQ = 256
TK = 256
HALO = 32
ROW_CHUNK = 64
V_AUG = 80
N_MAPS = 2 * N_HEADS
MASK_VALUE = -1e30
LOG2E = math.log2(math.e)
Q_SCALE = (QK_DIM ** -0.5) * LOG2E


def _layer_norm(x, g, b):
    mu = jnp.mean(x, axis=-1, keepdims=True)
    xc = x - mu
    var = jnp.mean(xc * xc, axis=-1, keepdims=True)
    return xc * lax.rsqrt(var + LN_EPS) * g + b


def _gelu_tanh(x):
    c = math.sqrt(2.0 / math.pi)
    return 0.5 * x * (1.0 + jnp.tanh(c * (x + 0.044715 * (x * x * x))))


def _sigmoid(x):
    return 1.0 / (1.0 + jnp.exp(-x))


def _mixer_in_kernel(layer, x_ref, emb_g_ref, emb_b_ref, w_in_ref, conv_w_ref, pw_w_ref, sg_w_ref,
                     sg_bias_ref, pool_w_ref, *rest):
    first = layer == 0
    conv_b, cln_g, cln_b, pw_b, sgln_g, sgln_b, pool_scale = (r[layer:layer + 1, :] for r in rest[:7])
    rest = rest[7:]
    if first:
        h_ref, q_ref, k_ref, v_ref, ymix_ref, glu_buf, shift_buf, pool_buf, win_buf, conv_buf, z_buf = rest
    else:
        q_ref, k_ref, v_ref, ymix_ref, glu_buf, shift_buf, pool_buf, win_buf, conv_buf, z_buf = rest
    tm = x_ref.shape[1]
    j = pl.program_id(1)

    @pl.when(j == 0)
    def _():
        glu_buf[0:HALO, :] = jnp.zeros((HALO, D_GROUP), F32)
        pool_buf[0:HALO, :] = jnp.zeros((HALO, D_GROUP), F32)

    @pl.when(j > 0)
    def _():
        glu_buf[0:HALO, :] = glu_buf[tm:tm + HALO, :]
        pool_buf[0:HALO, :] = pool_buf[tm:tm + HALO, :]

    x = x_ref[0]
    if first:
        h = _layer_norm(x, emb_g_ref[...], emb_b_ref[...])
        h_ref[0] = h
    else:
        h = x
    hb = h.astype(BF16)

    def proj(lo, hi):
        return jnp.dot(hb, w_in_ref[:, lo:hi], preferred_element_type=F32)

    za = proj(O_A, O_Q)
    glu_buf[HALO:HALO + tm, :] = za[:, :D_GROUP] * _sigmoid(za[:, D_GROUP:])
    q_ref[0] = (proj(O_Q, O_K) * Q_SCALE).astype(BF16)
    n_shift = HALO + tm - SUBLANES
    for s in range(1, SUBLANES):
        shift_buf[s - 1] = glu_buf[s:s + n_shift, :]

    def conv_rows(r_lo, r_hi):
        for r0 in range(r_lo, r_hi, ROW_CHUNK):
            acc = jnp.broadcast_to(conv_b, (ROW_CHUNK, D_GROUP))
            for tap in range(CONV_WIDTH):
                off = HALO + r0 - (CONV_WIDTH - 1) + tap
                base, s = off - off % SUBLANES, off % SUBLANES
                if s == 0:
                    rows = glu_buf[base:base + ROW_CHUNK, :]
                else:
                    assert base + ROW_CHUNK <= n_shift
                    rows = shift_buf[s - 1, base:base + ROW_CHUNK, :]
                acc = acc + conv_w_ref[tap:tap + 1, :] * rows
            conv_buf[r0:r0 + ROW_CHUNK, :] = acc

    k_ref[0] = proj(O_K, O_V).astype(BF16)
    v_ref[0] = proj(O_V, O_U).astype(BF16)
    conv_rows(0, tm // 2)
    z_buf[0] = proj(O_U, O_SV)
    z_buf[1] = proj(O_SV, O_P)
    conv_rows(tm // 2, tm)
    pool_buf[HALO:HALO + tm, :] = proj(O_P, D_IN)

    hc = _layer_norm(conv_buf[...], cln_g, cln_b)
    hc = hc * _sigmoid(hc)
    y_conv = jnp.dot(hc.astype(BF16), pw_w_ref[...], preferred_element_type=F32) + pw_b
    ymix_ref[0, :, 0:D_GROUP] = y_conv.astype(BF16)

    gu = _gelu_tanh(z_buf[0])
    gv = _layer_norm(_gelu_tanh(z_buf[1]), sgln_g, sgln_b).astype(BF16)
    lane = lax.broadcasted_iota(jnp.int32, (CHUNK, D_GROUP), 1)
    sg_w = sg_w_ref[...]
    sg_bias = sg_bias_ref[...]
    for c in range(tm // CHUNK):
        vc = gv[c * CHUNK:(c + 1) * CHUNK, :]
        stacked = jnp.concatenate(
            [jnp.where((lane >= hd * SG_HEAD_DIM) & (lane < (hd + 1) * SG_HEAD_DIM), vc,
                       jnp.zeros_like(vc)) for hd in range(N_SG_HEADS)], axis=0)
        mixed = jnp.dot(sg_w, stacked, preferred_element_type=F32) + sg_bias
        ymix_ref[0, c * CHUNK:(c + 1) * CHUNK, D_GROUP:2 * D_GROUP] = (
            gu[c * CHUNK:(c + 1) * CHUNK, :] * mixed).astype(BF16)

    assert POOL_WINDOWS == (2, 4, 8, 16) and HALO == 32
    n = HALO + tm
    win_buf[0, 8:n, :] = pool_buf[8:n, :] + pool_buf[7:n - 1, :]
    win_buf[1, 16:n, :] = win_buf[0, 16:n, :] + win_buf[0, 14:n - 2, :]
    win_buf[2, 24:n, :] = win_buf[1, 24:n, :] + win_buf[1, 20:n - 4, :]
    sum16 = win_buf[2, HALO:n, :] + win_buf[2, HALO - 8:n - 8, :]
    lane_r = lax.broadcasted_iota(jnp.int32, (tm, D_GROUP), 1)
    row_r = lax.broadcasted_iota(jnp.int32, (tm, D_GROUP), 0)
    group = lane_r // POOL_GROUP_DIM
    window = jnp.where(group == 0, POOL_WINDOWS[0],
                       jnp.where(group == 1, POOL_WINDOWS[1],
                                 jnp.where(group == 2, POOL_WINDOWS[2], POOL_WINDOWS[3])))
    total = jnp.where(group == 0, win_buf[0, HALO:n, :],
                      jnp.where(group == 1, win_buf[1, HALO:n, :],
                                jnp.where(group == 2, win_buf[2, HALO:n, :], sum16)))
    cnt = jnp.minimum(j * tm + row_r + 1, window).astype(F32)
    pooled = total / cnt - pool_buf[HALO:n, :]
    y_pool = jnp.dot(pooled.astype(BF16), pool_w_ref[...], preferred_element_type=F32)
    ymix_ref[0, :, 2 * D_GROUP:3 * D_GROUP] = (y_pool * pool_scale).astype(BF16)


def _const_spec(shape):
    return pl.BlockSpec(shape, lambda *_: (0,) * len(shape))


def _layer_spec(stacked, layer, **kwargs):
    zeros = (0,) * (stacked.ndim - 1)
    return pl.BlockSpec((None,) + stacked.shape[1:], lambda *_: (layer,) + zeros, **kwargs)


def _mixer_in(layer, x, emb_g, emb_b, stacked, vectors):
    first = layer == 0
    B, S, _ = x.shape
    tm = TM_IN
    tok = lambda width: pl.BlockSpec((1, tm, width), lambda b, j: (b, j, 0))
    out_shape = [jax.ShapeDtypeStruct((B, S, D_GROUP), BF16),
                 jax.ShapeDtypeStruct((B, S, D_GROUP), BF16),
                 jax.ShapeDtypeStruct((B, S, D_GROUP), BF16),
                 jax.ShapeDtypeStruct((B, S, 3 * D_GROUP), BF16)]
    out_specs = [tok(D_GROUP), tok(D_GROUP), tok(D_GROUP), tok(3 * D_GROUP)]
    if first:
        out_shape = [jax.ShapeDtypeStruct((B, S, D_MODEL), F32)] + out_shape
        out_specs = [tok(D_MODEL)] + out_specs
    return pl.pallas_call(
        functools.partial(_mixer_in_kernel, layer),
        grid=(B, S // tm),
        in_specs=([tok(D_MODEL), _const_spec(emb_g.shape), _const_spec(emb_b.shape)]
                  + [_layer_spec(p, layer) for p in stacked] + [_const_spec(p.shape) for p in vectors]),
        out_specs=out_specs,
        out_shape=out_shape,
        scratch_shapes=[pltpu.VMEM((HALO + tm, D_GROUP), F32),
                        pltpu.VMEM((SUBLANES - 1, HALO + tm - SUBLANES, D_GROUP), F32),
                        pltpu.VMEM((HALO + tm, D_GROUP), F32),
                        pltpu.VMEM((3, HALO + tm, D_GROUP), F32),
                        pltpu.VMEM((tm, D_GROUP), F32),
                        pltpu.VMEM((2, tm, D_GROUP), F32)],
        compiler_params=pltpu.CompilerParams(dimension_semantics=("parallel", "arbitrary"),
                                             vmem_limit_bytes=VMEM_LIMIT_BYTES),
        name="mixer_in",
    )(x, emb_g, emb_b, *stacked, *vectors)


def _diff_attn_kernel(layer, q_ref, k_ref, v_ref, table_ref, lq1_ref, lk1_ref, lq2_ref, lk2_ref, g_ref,
                      o_ref, bias_ref, vt_ref, qm_ref, s_ref, m_ref, a_ref, acc_ref):
    lam_init = 0.8 - 0.6 * math.exp(-0.3 * layer)
    i = pl.program_id(1)
    n_kv = k_ref.shape[1] // TK

    @pl.when((pl.program_id(0) == 0) & (i == 0))
    def _():
        for hd in range(N_HEADS):
            spread = jnp.broadcast_to(table_ref[hd:hd + 1, :], (TK, table_ref.shape[1]))
            sheared = pltpu.roll(spread, 0, axis=1, stride=1, stride_axis=0)
            bias_ref[0, :, hd * TQ:(hd + 1) * TQ] = sheared[:, 0:TQ]
            bias_ref[1, :, hd * TQ:(hd + 1) * TQ] = sheared[:, TK:TK + TQ]

    @pl.when(i == 0)
    def _():
        row = lax.broadcasted_iota(jnp.int32, (V_AUG - V_DIM, TK), 0)
        ones_rows = jnp.where(row == 0, 1.0, 0.0).astype(BF16)
        for jt in range(n_kv):
            vt = v_ref[0, jt * TK:(jt + 1) * TK, :].astype(F32).T
            for hd in range(N_HEADS):
                vt_ref[jt, hd, 0:V_DIM, :] = vt[hd * V_DIM:(hd + 1) * V_DIM, :].astype(BF16)
                vt_ref[jt, hd, V_DIM:V_AUG, :] = ones_rows

    q = q_ref[0].astype(F32)
    row = lax.broadcasted_iota(jnp.int32, (LANES, TQ), 0)
    for mp in range(2):
        qt = q[:, mp * LANES:(mp + 1) * LANES].T
        for hd in range(N_HEADS):
            keep = (row >= hd * QK_DIM) & (row < (hd + 1) * QK_DIM)
            qm_ref[hd * 2 + mp] = jnp.where(keep, qt, 0.0).astype(BF16)

    m_ref[1] = jnp.full(m_ref.shape[1:], MASK_VALUE, F32)
    acc_ref[...] = jnp.zeros(acc_ref.shape, F32)

    def scores_into(slot, t, bias, hd):
        rows = pl.ds(pl.multiple_of(t * TK, TK), TK)
        for mp in range(2):
            blk = hd * 2 + mp
            cols = slice(blk * TQ, (blk + 1) * TQ)
            st = jnp.dot(k_ref[0, rows, mp * LANES:(mp + 1) * LANES], qm_ref[blk],
                         preferred_element_type=F32)
            if bias is not None:
                st = st + bias_ref[bias, :, hd * TQ:(hd + 1) * TQ]
            s_ref[slot, :, cols] = st
            m_old = m_ref[1 - slot, :, cols]
            m_new = jnp.maximum(m_old, jnp.max(st, axis=0, keepdims=True))
            m_ref[slot, :, cols] = m_new
            a_ref[slot, :, cols] = jnp.exp2(m_old - m_new)

    def values_from(slot, t, hd):
        cols = slice(hd * 2 * TQ, (hd + 1) * 2 * TQ)
        p = jnp.exp2(s_ref[slot, :, cols] - m_ref[slot, :, cols]).astype(BF16)
        pv = jnp.dot(vt_ref[t, hd], p, preferred_element_type=F32)
        acc_ref[hd] = acc_ref[hd] * a_ref[slot, :, cols] + pv

    def stage(score_args, value_args):
        for hd in range(N_HEADS):
            if score_args is not None:
                scores_into(*score_args, hd)
            if value_args is not None:
                values_from(*value_args, hd)

    stage((0, i, 0), None)

    @pl.when(i >= 1)
    def _():
        stage((1, i - 1, 1), (0, i))

    def far_pair(k2, carry):
        t = i - 2 - 2 * k2
        stage((0, t, None), (1, t + 1))
        stage((1, t - 1, None), (0, t))
        return carry

    n_far = jnp.maximum(i - 1, 0)
    lax.fori_loop(0, n_far // 2, far_pair, 0)

    @pl.when((i >= 2) & (i % 2 == 0))
    def _():
        stage((0, 0, None), (1, 1))

    def last_values_and_finalize(slot):
        row = slice(layer, layer + 1)
        lam = (jnp.exp(jnp.sum(lq1_ref[row, :] * lk1_ref[row, :], keepdims=True))
               - jnp.exp(jnp.sum(lq2_ref[row, :] * lk2_ref[row, :], keepdims=True)) + lam_init)
        outs = []
        for hd in range(N_HEADS):
            values_from(slot, 0, hd)
            a = acc_ref[hd]
            o1 = a[0:V_DIM, 0:TQ] / a[V_DIM:V_DIM + 1, 0:TQ]
            o2 = a[0:V_DIM, TQ:2 * TQ] / a[V_DIM:V_DIM + 1, TQ:2 * TQ]
            o = o1 - lam * o2
            ms = jnp.mean(o * o, axis=0, keepdims=True)
            outs.append(o * lax.rsqrt(ms + LN_EPS) * (1.0 - lam_init))
        o_ref[0] = (jnp.concatenate(outs, axis=0).T * g_ref[row, :]).astype(BF16)

    @pl.when(i % 2 == 0)
    def _():
        last_values_and_finalize(0)

    @pl.when(i % 2 == 1)
    def _():
        last_values_and_finalize(1)


def _diff_attn(layer, q, k, v, table, vectors):
    B, S, _ = q.shape
    n_kv = S // TK
    return pl.pallas_call(
        functools.partial(_diff_attn_kernel, layer),
        grid=(B, S // TQ),
        in_specs=[pl.BlockSpec((1, TQ, D_GROUP), lambda b, i: (b, i, 0)),
                  pl.BlockSpec((1, S, D_GROUP), lambda b, i: (b, 0, 0)),
                  pl.BlockSpec((1, S, D_GROUP), lambda b, i: (b, 0, 0)),
                  _const_spec(table.shape)] + [_const_spec(p.shape) for p in vectors],
        out_specs=pl.BlockSpec((1, TQ, D_GROUP), lambda b, i: (b, i, 0)),
        out_shape=jax.ShapeDtypeStruct((B, S, D_GROUP), BF16),
        scratch_shapes=[pltpu.VMEM((2, TK, N_HEADS * TQ), F32),
                        pltpu.VMEM((n_kv, N_HEADS, V_AUG, TK), BF16),
                        pltpu.VMEM((N_MAPS, LANES, TQ), BF16),
                        pltpu.VMEM((2, TK, N_MAPS * TQ), F32),
                        pltpu.VMEM((2, 1, N_MAPS * TQ), F32),
                        pltpu.VMEM((2, 1, N_MAPS * TQ), F32),
                        pltpu.VMEM((N_HEADS, V_AUG, 2 * TQ), F32)],
        compiler_params=pltpu.CompilerParams(dimension_semantics=("arbitrary", "arbitrary"),
                                             vmem_limit_bytes=VMEM_LIMIT_BYTES),
        name="diff_attn",
    )(q, k, v, table, *vectors)


def _mixer_out_kernel(layer, h_ref, ymix_ref, ydiff_ref, w_out_ref, w1_ref, w2_ref, ln1_g_ref, ln1_b_ref,
                      ln2_g_ref, ln2_b_ref, o_ref):
    ln1_g, ln1_b, ln2_g, ln2_b = (r[layer:layer + 1, :] for r in (ln1_g_ref, ln1_b_ref, ln2_g_ref, ln2_b_ref))
    h = h_ref[...]
    ymix = ymix_ref[...]
    mix = (jnp.dot(ymix[:, 0:D_GROUP], w_out_ref[0:D_GROUP, :], preferred_element_type=F32)
           + jnp.dot(ydiff_ref[...], w_out_ref[D_GROUP:2 * D_GROUP, :], preferred_element_type=F32)
           + jnp.dot(ymix[:, D_GROUP:3 * D_GROUP], w_out_ref[2 * D_GROUP:4 * D_GROUP, :],
                     preferred_element_type=F32))
    h1 = _layer_norm(ALPHA * h + mix, ln1_g, ln1_b)
    a = jnp.dot(h1.astype(BF16), w1_ref[...], preferred_element_type=F32)
    a = jnp.maximum(a, 0.0)
    a = (a * a).astype(BF16)
    ff = jnp.dot(a, w2_ref[...], preferred_element_type=F32)
    o_ref[...] = _layer_norm(ALPHA * h1 + ff, ln2_g, ln2_b)


def _mixer_out(layer, h, ymix, ydiff, stacked, vectors):
    T = h.shape[0]
    tm = TM_OUT
    tok = lambda width: pl.BlockSpec((tm, width), lambda t: (t, 0))
    return pl.pallas_call(
        functools.partial(_mixer_out_kernel, layer),
        grid=(T // tm,),
        in_specs=([tok(D_MODEL), tok(3 * D_GROUP), tok(D_GROUP)]
                  + [_layer_spec(p, layer, pipeline_mode=pl.Buffered(1)) for p in stacked]
                  + [_const_spec(p.shape) for p in vectors]),
        out_specs=tok(D_MODEL),
        out_shape=jax.ShapeDtypeStruct((T, D_MODEL), F32),
        compiler_params=pltpu.CompilerParams(dimension_semantics=("parallel",),
                                             vmem_limit_bytes=VMEM_LIMIT_BYTES),
        name="mixer_out",
    )(h, ymix, ydiff, *stacked, *vectors)


def _t5_bucket(n):
    max_exact = N_BUCKETS // 2
    large = max_exact + (jnp.log(jnp.maximum(n, 1).astype(F32) / max_exact)
                         / math.log(MAX_DISTANCE / max_exact) * (N_BUCKETS - max_exact)).astype(jnp.int32)
    large = jnp.minimum(large, N_BUCKETS - 1)
    return jnp.where(n < max_exact, n, large)


def _bias_table(rel_bias):
    assert TK >= MAX_DISTANCE, "tiles two or more behind the diagonal must see one constant bias"
    dist = jnp.arange(TQ + TK)
    onehot = (_t5_bucket(dist)[:, None] == jnp.arange(N_BUCKETS)[None, :]).astype(F32)
    rb = rel_bias.astype(F32)
    vals = jnp.sum(onehot[:, :, None] * (rb - rb[N_BUCKETS - 1])[None, :, :], axis=1) * LOG2E
    return jnp.concatenate([vals, jnp.full((TQ + TK, N_HEADS), MASK_VALUE, F32)], axis=0).T


def _block_diag(w):
    n_layers, g, c, d = w.shape
    out = jnp.zeros((n_layers, g * c, g * d), w.dtype)
    for i in range(g):
        out = out.at[:, i * c:(i + 1) * c, i * d:(i + 1) * d].set(w[:, i])
    return out


def kernel(x, emb_ln_g, emb_ln_b, rel_bias, w_in, conv_w, conv_b, conv_ln_g, conv_ln_b, conv_pw_w,
           conv_pw_b, lam_q1, lam_k1, lam_q2, lam_k2, diff_norm_g, sg_ln_g, sg_ln_b, sg_w, sg_b, pool_w,
           pool_scale, w_out, ln1_g, ln1_b, w_mlp1, w_mlp2, ln2_g, ln2_b):
    B, S, _ = x.shape
    assert S % TM_IN == 0 and (B * S) % TM_OUT == 0 and S % TQ == 0 and TQ == TK
    small = [conv_b, conv_ln_g, conv_ln_b, conv_pw_b, sg_ln_g, sg_ln_b, pool_scale, lam_q1, lam_k1,
             lam_q2, lam_k2, diff_norm_g, ln1_g, ln1_b, ln2_g, ln2_b]
    assert all(p.dtype == F32 and p.shape[0] == DEPTH and p.ndim == 2 for p in small)
    tril = jnp.tril(jnp.ones((CHUNK, CHUNK), F32))
    sg_w_cat = jnp.transpose(sg_w * tril, (0, 2, 1, 3)).reshape(DEPTH, CHUNK, N_SG_HEADS * CHUNK)
    sg_bias = jnp.repeat(jnp.swapaxes(sg_b, 1, 2), SG_HEAD_DIM, axis=2).astype(F32)
    in_matrices = [w_in.astype(BF16), conv_w.astype(F32), conv_pw_w.astype(BF16), sg_w_cat.astype(BF16),
                   sg_bias, _block_diag(pool_w).astype(BF16)]
    in_vectors = [conv_b, conv_ln_g, conv_ln_b, conv_pw_b, sg_ln_g, sg_ln_b, pool_scale]
    attn_vectors = [lam_q1, lam_k1, lam_q2, lam_k2, jnp.tile(diff_norm_g, (1, N_HEADS))]
    out_matrices = [w_out.astype(BF16), w_mlp1.astype(BF16), w_mlp2.astype(BF16)]
    out_vectors = [ln1_g, ln1_b, ln2_g, ln2_b]
    table = _bias_table(rel_bias)
    emb_g, emb_b = emb_ln_g.reshape(1, -1).astype(F32), emb_ln_b.reshape(1, -1).astype(F32)

    h = x
    for l in range(DEPTH):
        outs = _mixer_in(l, h, emb_g, emb_b, in_matrices, in_vectors)
        if l == 0:
            h, q, k, v, ymix = outs
        else:
            q, k, v, ymix = outs
        ydiff = _diff_attn(l, q, k, v, table, attn_vectors)
        h = _mixer_out(l, h.reshape(B * S, D_MODEL), ymix.reshape(B * S, 3 * D_GROUP),
                       ydiff.reshape(B * S, D_GROUP), out_matrices, out_vectors).reshape(B, S, D_MODEL)
    return h
```

```python
import functools
import math

import jax
import jax.numpy as jnp
import numpy as np
from jax import lax
from jax.experimental import pallas as pl
from jax.experimental.pallas import tpu as pltpu

F32 = jnp.float32
BF16 = jnp.bfloat16

D_MODEL = 1024
DEPTH = 2
D_GROUP = 256
CONV_WIDTH = 31
N_HEADS = 4
QK_DIM = 32
V_DIM = 64
N_SG_HEADS = 4
SG_HEAD_DIM = 64
CHUNK = 128
POOL_WINDOWS = (2, 4, 8, 16)
POOL_GROUP_DIM = 64
D_FF = 4096
N_BUCKETS = 32
MAX_DISTANCE = 128
LN_EPS = 1e-5
ALPHA = (2.0 * DEPTH) ** 0.25
D_IN = 2048
O_A, O_Q, O_K, O_V, O_U, O_SV, O_P = 0, 512, 768, 1024, 1280, 1536, 1792

LANES = 128
SUBLANES = 8
VMEM_LIMIT_BYTES = 56 * 1024 * 1024

TM_IN = 1024
TM_OUT = 512
OUT_ROWS = 256
TQ = 256
TK = 256
HALO = 32
ROW_CHUNK = 64
V_AUG = 80
N_MAPS = 2 * N_HEADS
MASK_VALUE = -1e30
LOG2E = math.log2(math.e)
Q_SCALE = (QK_DIM ** -0.5) * LOG2E


def _layer_norm(x, g, b):
    mu = jnp.mean(x, axis=-1, keepdims=True)
    xc = x - mu
    var = jnp.mean(xc * xc, axis=-1, keepdims=True)
    return xc * lax.rsqrt(var + LN_EPS) * g + b


def _gelu_tanh(x):
    c = math.sqrt(2.0 / math.pi)
    return 0.5 * x * (1.0 + jnp.tanh(c * (x + 0.044715 * (x * x * x))))


def _sigmoid(x):
    return 1.0 / (1.0 + jnp.exp(-x))


def _runtime_zero_bits(v):
    bits = pltpu.bitcast(v, jnp.int32)
    return lax.shift_right_logical(lax.shift_right_logical(bits, 16), 16)


def _runtime_zero(v):
    return _runtime_zero_bits(v).astype(F32)


def _mixer_in_kernel(layer, x_ref, emb_g_ref, emb_b_ref, w_in_ref, conv_w_ref, pw_w_ref, sg_w_ref,
                     sg_bias_ref, pool_w_ref, *rest):
    first = layer == 0
    conv_b, cln_g, cln_b, pw_b, sgln_g, sgln_b, pool_scale = (r[layer:layer + 1, :] for r in rest[:7])
    rest = rest[7:]
    if first:
        h_ref, *rest = rest
    q_ref, k_ref, v_ref, ymix_ref, glu_buf, shift_buf, pool_buf, win_buf, conv_buf, gu_buf, gv_buf = rest
    tm = x_ref.shape[1]
    j = pl.program_id(1)

    @pl.when(j == 0)
    def _():
        glu_buf[0:HALO, :] = jnp.zeros((HALO, D_GROUP), F32)
        pool_buf[0:HALO, :] = jnp.zeros((HALO, D_GROUP), F32)

    @pl.when(j > 0)
    def _():
        glu_buf[0:HALO, :] = glu_buf[tm:tm + HALO, :]
        pool_buf[0:HALO, :] = pool_buf[tm:tm + HALO, :]

    x = x_ref[0]
    if first:
        h = _layer_norm(x, emb_g_ref[...], emb_b_ref[...])
        h_ref[0] = h
    else:
        h = x
    hb = h.astype(BF16)

    def first_vreg(value):
        return value[0:SUBLANES, 0:LANES]

    def proj(lo, hi, after=None):
        w = w_in_ref[:, lo:hi]
        if after is not None:
            bits = pltpu.bitcast(w, jnp.int32)
            zero = _runtime_zero_bits(after)
            bits = bits | jnp.tile(zero, (bits.shape[0] // SUBLANES, bits.shape[1] // LANES))
            w = pltpu.bitcast(bits, BF16)
        return jnp.dot(hb, w, preferred_element_type=F32)

    n = HALO + tm
    n_shift = n - SUBLANES
    quarter = tm // 4

    def conv_rows(r_lo, r_hi, after):
        for r0 in range(r_lo, r_hi, ROW_CHUNK):
            acc = jnp.broadcast_to(conv_b, (ROW_CHUNK, D_GROUP))
            if r0 == r_lo:
                acc = acc + jnp.tile(_runtime_zero(after), (ROW_CHUNK // SUBLANES, D_GROUP // LANES))
            for tap in range(CONV_WIDTH):
                off = HALO + r0 - (CONV_WIDTH - 1) + tap
                base, s = off - off % SUBLANES, off % SUBLANES
                if s == 0:
                    rows = glu_buf[base:base + ROW_CHUNK, :]
                else:
                    assert base + ROW_CHUNK <= n_shift
                    rows = shift_buf[s - 1, base:base + ROW_CHUNK, :]
                acc = acc + conv_w_ref[tap:tap + 1, :] * rows
            conv_buf[r0:r0 + ROW_CHUNK, :] = acc
        return first_vreg(acc)

    za = proj(O_A, O_Q)
    glu_buf[HALO:n, :] = za[:, :D_GROUP] * _sigmoid(za[:, D_GROUP:])
    q_ref[0] = (proj(O_Q, O_K) * Q_SCALE).astype(BF16)
    for s in range(1, SUBLANES):
        shift_buf[s - 1] = glu_buf[s:s + n_shift, :]

    zk = proj(O_K, O_V)
    k_ref[0] = zk.astype(BF16)
    c0 = conv_rows(0, quarter, first_vreg(zk))
    zv = proj(O_V, O_U)
    v_ref[0] = zv.astype(BF16)
    c1 = conv_rows(quarter, 2 * quarter, first_vreg(zv))
    zu = proj(O_U, O_SV, after=c0)
    gu_buf[...] = zu
    c2 = conv_rows(2 * quarter, 3 * quarter, first_vreg(zu))
    zsv = proj(O_SV, O_P, after=c1)
    gv_buf[...] = zsv
    conv_rows(3 * quarter, tm, first_vreg(zsv))
    pool_buf[HALO:n, :] = proj(O_P, D_IN, after=c2)

    hc = _layer_norm(conv_buf[...], cln_g, cln_b)
    hc = hc * _sigmoid(hc)
    y_conv = jnp.dot(hc.astype(BF16), pw_w_ref[...], preferred_element_type=F32) + pw_b
    ymix_ref[0, :, 0:D_GROUP] = y_conv.astype(BF16)

    gu = _gelu_tanh(gu_buf[...])
    gv = _layer_norm(_gelu_tanh(gv_buf[...]), sgln_g, sgln_b).astype(BF16)
    lane = lax.broadcasted_iota(jnp.int32, (CHUNK, D_GROUP), 1)
    sg_w = sg_w_ref[...]
    sg_bias = sg_bias_ref[...]
    for c in range(tm // CHUNK):
        vc = gv[c * CHUNK:(c + 1) * CHUNK, :]
        stacked = jnp.concatenate(
            [jnp.where((lane >= hd * SG_HEAD_DIM) & (lane < (hd + 1) * SG_HEAD_DIM), vc,
                       jnp.zeros_like(vc)) for hd in range(N_SG_HEADS)], axis=0)
        mixed = jnp.dot(sg_w, stacked, preferred_element_type=F32) + sg_bias
        ymix_ref[0, c * CHUNK:(c + 1) * CHUNK, D_GROUP:2 * D_GROUP] = (
            gu[c * CHUNK:(c + 1) * CHUNK, :] * mixed).astype(BF16)

    assert POOL_WINDOWS == (2, 4, 8, 16) and HALO == 32
    win_buf[0, 8:n, :] = pool_buf[8:n, :] + pool_buf[7:n - 1, :]
    win_buf[1, 16:n, :] = win_buf[0, 16:n, :] + win_buf[0, 14:n - 2, :]
    win_buf[2, 24:n, :] = win_buf[1, 24:n, :] + win_buf[1, 20:n - 4, :]
    sum16 = win_buf[2, HALO:n, :] + win_buf[2, HALO - 8:n - 8, :]
    lane_r = lax.broadcasted_iota(jnp.int32, (tm, D_GROUP), 1)
    row_r = lax.broadcasted_iota(jnp.int32, (tm, D_GROUP), 0)
    group = lane_r // POOL_GROUP_DIM
    window = jnp.where(group == 0, POOL_WINDOWS[0],
                       jnp.where(group == 1, POOL_WINDOWS[1],
                                 jnp.where(group == 2, POOL_WINDOWS[2], POOL_WINDOWS[3])))
    total = jnp.where(group == 0, win_buf[0, HALO:n, :],
                      jnp.where(group == 1, win_buf[1, HALO:n, :],
                                jnp.where(group == 2, win_buf[2, HALO:n, :], sum16)))
    cnt = jnp.minimum(j * tm + row_r + 1, window).astype(F32)
    pooled = total / cnt - pool_buf[HALO:n, :]
    y_pool = jnp.dot(pooled.astype(BF16), pool_w_ref[...], preferred_element_type=F32)
    ymix_ref[0, :, 2 * D_GROUP:3 * D_GROUP] = (y_pool * pool_scale).astype(BF16)


def _const_spec(shape):
    return pl.BlockSpec(shape, lambda *_: (0,) * len(shape))


def _layer_spec(stacked, layer, **kwargs):
    zeros = (0,) * (stacked.ndim - 1)
    return pl.BlockSpec((None,) + stacked.shape[1:], lambda *_: (layer,) + zeros, **kwargs)


def _mixer_in(layer, x, emb_g, emb_b, stacked, vectors):
    first = layer == 0
    B, S, _ = x.shape
    tm = TM_IN
    tok = lambda width: pl.BlockSpec((1, tm, width), lambda b, j: (b, j, 0))
    out_shape = [jax.ShapeDtypeStruct((B, S, D_GROUP), BF16),
                 jax.ShapeDtypeStruct((B, S, D_GROUP), BF16),
                 jax.ShapeDtypeStruct((B, S, D_GROUP), BF16),
                 jax.ShapeDtypeStruct((B, S, 3 * D_GROUP), BF16)]
    out_specs = [tok(D_GROUP), tok(D_GROUP), tok(D_GROUP), tok(3 * D_GROUP)]
    if first:
        out_shape = [jax.ShapeDtypeStruct((B, S, D_MODEL), F32)] + out_shape
        out_specs = [tok(D_MODEL)] + out_specs
    return pl.pallas_call(
        functools.partial(_mixer_in_kernel, layer),
        grid=(B, S // tm),
        in_specs=([tok(D_MODEL), _const_spec(emb_g.shape), _const_spec(emb_b.shape)]
                  + [_layer_spec(p, layer) for p in stacked] + [_const_spec(p.shape) for p in vectors]),
        out_specs=out_specs,
        out_shape=out_shape,
        scratch_shapes=[pltpu.VMEM((HALO + tm, D_GROUP), F32),
                        pltpu.VMEM((SUBLANES - 1, HALO + tm - SUBLANES, D_GROUP), F32),
                        pltpu.VMEM((HALO + tm, D_GROUP), F32),
                        pltpu.VMEM((3, HALO + tm, D_GROUP), F32),
                        pltpu.VMEM((tm, D_GROUP), F32),
                        pltpu.VMEM((tm, D_GROUP), F32),
                        pltpu.VMEM((tm, D_GROUP), F32)],
        compiler_params=pltpu.CompilerParams(dimension_semantics=("parallel", "arbitrary"),
                                             vmem_limit_bytes=VMEM_LIMIT_BYTES),
        name="mixer_in",
    )(x, emb_g, emb_b, *stacked, *vectors)


def _diff_attn_kernel(layer, q_ref, k_ref, v_ref, table_ref, lq1_ref, lk1_ref, lq2_ref, lk2_ref, g_ref,
                      o_ref, bias_ref, vt_ref, qm_ref, s_ref, m_ref, a_ref, acc_ref):
    lam_init = 0.8 - 0.6 * math.exp(-0.3 * layer)
    i = pl.program_id(1)
    n_kv = k_ref.shape[1] // TK

    @pl.when((pl.program_id(0) == 0) & (i == 0))
    def _():
        for hd in range(N_HEADS):
            spread = jnp.broadcast_to(table_ref[hd:hd + 1, :], (TK, table_ref.shape[1]))
            sheared = pltpu.roll(spread, 0, axis=1, stride=1, stride_axis=0)
            bias_ref[0, :, hd * TQ:(hd + 1) * TQ] = sheared[:, 0:TQ]
            bias_ref[1, :, hd * TQ:(hd + 1) * TQ] = sheared[:, TK:TK + TQ]

    @pl.when(i == 0)
    def _():
        row = lax.broadcasted_iota(jnp.int32, (V_AUG - V_DIM, TK), 0)
        ones_rows = jnp.where(row == 0, 1.0, 0.0).astype(BF16)
        for jt in range(n_kv):
            vt = v_ref[0, jt * TK:(jt + 1) * TK, :].astype(F32).T
            for hd in range(N_HEADS):
                vt_ref[jt, hd, 0:V_DIM, :] = vt[hd * V_DIM:(hd + 1) * V_DIM, :].astype(BF16)
                vt_ref[jt, hd, V_DIM:V_AUG, :] = ones_rows

    q = q_ref[0].astype(F32)
    row = lax.broadcasted_iota(jnp.int32, (LANES, TQ), 0)
    for mp in range(2):
        qt = q[:, mp * LANES:(mp + 1) * LANES].T
        for hd in range(N_HEADS):
            keep = (row >= hd * QK_DIM) & (row < (hd + 1) * QK_DIM)
            qm_ref[hd * 2 + mp] = jnp.where(keep, qt, 0.0).astype(BF16)

    m_ref[1] = jnp.full(m_ref.shape[1:], MASK_VALUE, F32)
    acc_ref[...] = jnp.zeros(acc_ref.shape, F32)

    def scores_into(slot, t, bias, hd):
        rows = pl.ds(pl.multiple_of(t * TK, TK), TK)
        for mp in range(2):
            blk = hd * 2 + mp
            cols = slice(blk * TQ, (blk + 1) * TQ)
            st = jnp.dot(k_ref[0, rows, mp * LANES:(mp + 1) * LANES], qm_ref[blk],
                         preferred_element_type=F32)
            if bias is not None:
                st = st + bias_ref[bias, :, hd * TQ:(hd + 1) * TQ]
            s_ref[slot, :, cols] = st
            m_old = m_ref[1 - slot, :, cols]
            m_new = jnp.maximum(m_old, jnp.max(st, axis=0, keepdims=True))
            m_ref[slot, :, cols] = m_new
            a_ref[slot, :, cols] = jnp.exp2(m_old - m_new)

    def values_from(slot, t, hd):
        cols = slice(hd * 2 * TQ, (hd + 1) * 2 * TQ)
        p = jnp.exp2(s_ref[slot, :, cols] - m_ref[slot, :, cols]).astype(BF16)
        pv = jnp.dot(vt_ref[t, hd], p, preferred_element_type=F32)
        acc_ref[hd] = acc_ref[hd] * a_ref[slot, :, cols] + pv

    def stage(score_args, value_args):
        for hd in range(N_HEADS):
            if score_args is not None:
                scores_into(*score_args, hd)
            if value_args is not None:
                values_from(*value_args, hd)

    stage((0, i, 0), None)

    @pl.when(i >= 1)
    def _():
        stage((1, i - 1, 1), (0, i))

    def far_pair(k2, carry):
        t = i - 2 - 2 * k2
        stage((0, t, None), (1, t + 1))
        stage((1, t - 1, None), (0, t))
        return carry

    n_far = jnp.maximum(i - 1, 0)
    lax.fori_loop(0, n_far // 2, far_pair, 0)

    @pl.when((i >= 2) & (i % 2 == 0))
    def _():
        stage((0, 0, None), (1, 1))

    def last_values_and_finalize(slot):
        row = slice(layer, layer + 1)
        lam = (jnp.exp(jnp.sum(lq1_ref[row, :] * lk1_ref[row, :], keepdims=True))
               - jnp.exp(jnp.sum(lq2_ref[row, :] * lk2_ref[row, :], keepdims=True)) + lam_init)
        outs = []
        for hd in range(N_HEADS):
            values_from(slot, 0, hd)
            a = acc_ref[hd]
            o1 = a[0:V_DIM, 0:TQ] / a[V_DIM:V_DIM + 1, 0:TQ]
            o2 = a[0:V_DIM, TQ:2 * TQ] / a[V_DIM:V_DIM + 1, TQ:2 * TQ]
            o = o1 - lam * o2
            ms = jnp.mean(o * o, axis=0, keepdims=True)
            outs.append(o * lax.rsqrt(ms + LN_EPS) * (1.0 - lam_init))
        o_ref[0] = (jnp.concatenate(outs, axis=0).T * g_ref[row, :]).astype(BF16)

    @pl.when(i % 2 == 0)
    def _():
        last_values_and_finalize(0)

    @pl.when(i % 2 == 1)
    def _():
        last_values_and_finalize(1)


def _diff_attn(layer, q, k, v, table, vectors):
    B, S, _ = q.shape
    n_kv = S // TK
    return pl.pallas_call(
        functools.partial(_diff_attn_kernel, layer),
        grid=(B, S // TQ),
        in_specs=[pl.BlockSpec((1, TQ, D_GROUP), lambda b, i: (b, i, 0)),
                  pl.BlockSpec((1, S, D_GROUP), lambda b, i: (b, 0, 0)),
                  pl.BlockSpec((1, S, D_GROUP), lambda b, i: (b, 0, 0)),
                  _const_spec(table.shape)] + [_const_spec(p.shape) for p in vectors],
        out_specs=pl.BlockSpec((1, TQ, D_GROUP), lambda b, i: (b, i, 0)),
        out_shape=jax.ShapeDtypeStruct((B, S, D_GROUP), BF16),
        scratch_shapes=[pltpu.VMEM((2, TK, N_HEADS * TQ), F32),
                        pltpu.VMEM((n_kv, N_HEADS, V_AUG, TK), BF16),
                        pltpu.VMEM((N_MAPS, LANES, TQ), BF16),
                        pltpu.VMEM((2, TK, N_MAPS * TQ), F32),
                        pltpu.VMEM((2, 1, N_MAPS * TQ), F32),
                        pltpu.VMEM((2, 1, N_MAPS * TQ), F32),
                        pltpu.VMEM((N_HEADS, V_AUG, 2 * TQ), F32)],
        compiler_params=pltpu.CompilerParams(dimension_semantics=("arbitrary", "arbitrary"),
                                             vmem_limit_bytes=VMEM_LIMIT_BYTES),
        name="diff_attn",
    )(q, k, v, table, *vectors)


def _mixer_out_kernel(layer, h_ref, ymix_ref, ydiff_ref, w_out_ref, w1_ref, w2_ref, ln1_g_ref, ln1_b_ref,
                      ln2_g_ref, ln2_b_ref, o_ref):
    ln1_g, ln1_b, ln2_g, ln2_b = (r[layer:layer + 1, :] for r in (ln1_g_ref, ln1_b_ref, ln2_g_ref, ln2_b_ref))
    tm = h_ref.shape[0]
    halves = [slice(r, r + OUT_ROWS) for r in range(0, tm, OUT_ROWS)]
    mix = []
    for rows in halves:
        ymix = ymix_ref[rows, :]
        mix.append(jnp.dot(ymix[:, 0:D_GROUP], w_out_ref[0:D_GROUP, :], preferred_element_type=F32)
                   + jnp.dot(ydiff_ref[rows, :], w_out_ref[D_GROUP:2 * D_GROUP, :],
                             preferred_element_type=F32)
                   + jnp.dot(ymix[:, D_GROUP:3 * D_GROUP], w_out_ref[2 * D_GROUP:4 * D_GROUP, :],
                             preferred_element_type=F32))
    h1 = [_layer_norm(ALPHA * h_ref[rows, :] + m, ln1_g, ln1_b) for rows, m in zip(halves, mix)]
    act = []
    for x in h1:
        a = jnp.maximum(jnp.dot(x.astype(BF16), w1_ref[...], preferred_element_type=F32), 0.0)
        act.append((a * a).astype(BF16))
    ff = [jnp.dot(a, w2_ref[...], preferred_element_type=F32) for a in act]
    for rows, x, f in zip(halves, h1, ff):
        o_ref[rows, :] = _layer_norm(ALPHA * x + f, ln2_g, ln2_b)


def _mixer_out(layer, h, ymix, ydiff, stacked, vectors):
    T = h.shape[0]
    tm = TM_OUT
    tok = lambda width: pl.BlockSpec((tm, width), lambda t: (t, 0))
    return pl.pallas_call(
        functools.partial(_mixer_out_kernel, layer),
        grid=(T // tm,),
        in_specs=([tok(D_MODEL), tok(3 * D_GROUP), tok(D_GROUP)]
                  + [_layer_spec(p, layer, pipeline_mode=pl.Buffered(1)) for p in stacked]
                  + [_const_spec(p.shape) for p in vectors]),
        out_specs=tok(D_MODEL),
        out_shape=jax.ShapeDtypeStruct((T, D_MODEL), F32),
        compiler_params=pltpu.CompilerParams(dimension_semantics=("parallel",),
                                             vmem_limit_bytes=VMEM_LIMIT_BYTES),
        name="mixer_out",
    )(h, ymix, ydiff, *stacked, *vectors)


def _t5_bucket(n):
    max_exact = N_BUCKETS // 2
    large = max_exact + (jnp.log(jnp.maximum(n, 1).astype(F32) / max_exact)
                         / math.log(MAX_DISTANCE / max_exact) * (N_BUCKETS - max_exact)).astype(jnp.int32)
    large = jnp.minimum(large, N_BUCKETS - 1)
    return jnp.where(n < max_exact, n, large)


def _bias_table(rel_bias):
    assert TK >= MAX_DISTANCE, "tiles two or more behind the diagonal must see one constant bias"
    dist = jnp.arange(TQ + TK)
    onehot = (_t5_bucket(dist)[:, None] == jnp.arange(N_BUCKETS)[None, :]).astype(F32)
    rb = rel_bias.astype(F32)
    vals = jnp.sum(onehot[:, :, None] * (rb - rb[N_BUCKETS - 1])[None, :, :], axis=1) * LOG2E
    return jnp.concatenate([vals, jnp.full((TQ + TK, N_HEADS), MASK_VALUE, F32)], axis=0).T


def _block_diag(w):
    n_layers, g, c, d = w.shape
    out = jnp.zeros((n_layers, g * c, g * d), w.dtype)
    for i in range(g):
        out = out.at[:, i * c:(i + 1) * c, i * d:(i + 1) * d].set(w[:, i])
    return out


def kernel(x, emb_ln_g, emb_ln_b, rel_bias, w_in, conv_w, conv_b, conv_ln_g, conv_ln_b, conv_pw_w,
           conv_pw_b, lam_q1, lam_k1, lam_q2, lam_k2, diff_norm_g, sg_ln_g, sg_ln_b, sg_w, sg_b, pool_w,
           pool_scale, w_out, ln1_g, ln1_b, w_mlp1, w_mlp2, ln2_g, ln2_b):
    B, S, _ = x.shape
    assert S % TM_IN == 0 and (B * S) % TM_OUT == 0 and S % TQ == 0 and TQ == TK
    small = [conv_b, conv_ln_g, conv_ln_b, conv_pw_b, sg_ln_g, sg_ln_b, pool_scale, lam_q1, lam_k1,
             lam_q2, lam_k2, diff_norm_g, ln1_g, ln1_b, ln2_g, ln2_b]
    assert all(p.dtype == F32 and p.shape[0] == DEPTH and p.ndim == 2 for p in small)
    tril = jnp.tril(jnp.ones((CHUNK, CHUNK), F32))
    sg_w_cat = jnp.transpose(sg_w * tril, (0, 2, 1, 3)).reshape(DEPTH, CHUNK, N_SG_HEADS * CHUNK)
    sg_bias = jnp.repeat(jnp.swapaxes(sg_b, 1, 2), SG_HEAD_DIM, axis=2).astype(F32)
    in_matrices = [w_in.astype(BF16), conv_w.astype(F32), conv_pw_w.astype(BF16), sg_w_cat.astype(BF16),
                   sg_bias, _block_diag(pool_w).astype(BF16)]
    in_vectors = [conv_b, conv_ln_g, conv_ln_b, conv_pw_b, sg_ln_g, sg_ln_b, pool_scale]
    attn_vectors = [lam_q1, lam_k1, lam_q2, lam_k2, jnp.tile(diff_norm_g, (1, N_HEADS))]
    out_matrices = [w_out.astype(BF16), w_mlp1.astype(BF16), w_mlp2.astype(BF16)]
    out_vectors = [ln1_g, ln1_b, ln2_g, ln2_b]
    table = _bias_table(rel_bias)
    emb_g, emb_b = emb_ln_g.reshape(1, -1).astype(F32), emb_ln_b.reshape(1, -1).astype(F32)

    h = x
    for l in range(DEPTH):
        outs = _mixer_in(l, h, emb_g, emb_b, in_matrices, in_vectors)
        if l == 0:
            h, q, k, v, ymix = outs
        else:
            q, k, v, ymix = outs
        ydiff = _diff_attn(l, q, k, v, table, attn_vectors)
        h = _mixer_out(l, h.reshape(B * S, D_MODEL), ymix.reshape(B * S, 3 * D_GROUP),
                       ydiff.reshape(B * S, D_GROUP), out_matrices, out_vectors).reshape(B, S, D_MODEL)
    return h
```

```python
import functools
import math

import jax
import jax.numpy as jnp
import numpy as np
from jax import lax
from jax.experimental import pallas as pl
from jax.experimental.pallas import tpu as pltpu

F32 = jnp.float32
BF16 = jnp.bfloat16

D_MODEL = 1024
DEPTH = 2
D_GROUP = 256
CONV_WIDTH = 31
N_HEADS = 4
QK_DIM = 32
V_DIM = 64
N_SG_HEADS = 4
SG_HEAD_DIM = 64
CHUNK = 128
POOL_WINDOWS = (2, 4, 8, 16)
POOL_GROUP_DIM = 64
D_FF = 4096
N_BUCKETS = 32
MAX_DISTANCE = 128
LN_EPS = 1e-5
ALPHA = (2.0 * DEPTH) ** 0.25
D_IN = 2048
O_A, O_Q, O_K, O_V, O_U, O_SV, O_P = 0, 512, 768, 1024, 1280, 1536, 1792

LANES = 128
SUBLANES = 8
VMEM_LIMIT_BYTES = 56 * 1024 * 1024

TM_IN = 1024
TM_OUT = 512
OUT_ROWS = 256
TQ = 256
TK = 256
HALO = 32
ROW_CHUNK = 64
V_AUG = 80
N_MAPS = 2 * N_HEADS
MASK_VALUE = -1e30
LOG2E = math.log2(math.e)
Q_SCALE = (QK_DIM ** -0.5) * LOG2E


def _layer_norm(x, g, b):
    mu = jnp.mean(x, axis=-1, keepdims=True)
    xc = x - mu
    var = jnp.mean(xc * xc, axis=-1, keepdims=True)
    return xc * lax.rsqrt(var + LN_EPS) * g + b


def _gelu_tanh(x):
    c = math.sqrt(2.0 / math.pi)
    return 0.5 * x * (1.0 + jnp.tanh(c * (x + 0.044715 * (x * x * x))))


def _sigmoid(x):
    return 1.0 / (1.0 + jnp.exp(-x))


def _runtime_zero_bits(v):
    bits = pltpu.bitcast(v, jnp.int32)
    return lax.shift_right_logical(lax.shift_right_logical(bits, 16), 16)


def _runtime_zero(v):
    return _runtime_zero_bits(v).astype(F32)


def _mixer_in_kernel(layer, x_ref, emb_g_ref, emb_b_ref, w_in_ref, conv_w_ref, pw_w_ref, sg_w_ref,
                     sg_bias_ref, pool_w_ref, *rest):
    first = layer == 0
    conv_b, cln_g, cln_b, pw_b, sgln_g, sgln_b, pool_scale = (r[layer:layer + 1, :] for r in rest[:7])
    rest = rest[7:]
    if first:
        h_ref, *rest = rest
    q_ref, k_ref, v_ref, ymix_ref, glu_buf, shift_buf, pool_buf, win_buf, conv_buf, gu_buf, gv_buf = rest
    tm = x_ref.shape[1]
    j = pl.program_id(1)

    @pl.when(j == 0)
    def _():
        glu_buf[0:HALO, :] = jnp.zeros((HALO, D_GROUP), F32)
        pool_buf[0:HALO, :] = jnp.zeros((HALO, D_GROUP), F32)

    @pl.when(j > 0)
    def _():
        glu_buf[0:HALO, :] = glu_buf[tm:tm + HALO, :]
        pool_buf[0:HALO, :] = pool_buf[tm:tm + HALO, :]

    x = x_ref[0]
    if first:
        h = _layer_norm(x, emb_g_ref[...], emb_b_ref[...])
        h_ref[0] = h
    else:
        h = x
    hb = h.astype(BF16)

    def first_vreg(value):
        return value[0:SUBLANES, 0:LANES]

    def proj(lo, hi, after=None):
        w = w_in_ref[:, lo:hi]
        if after is not None:
            bits = pltpu.bitcast(w, jnp.int32)
            zero = _runtime_zero_bits(after)
            bits = bits | jnp.tile(zero, (bits.shape[0] // SUBLANES, bits.shape[1] // LANES))
            w = pltpu.bitcast(bits, BF16)
        return jnp.dot(hb, w, preferred_element_type=F32)

    n = HALO + tm
    n_shift = n - SUBLANES
    quarter = tm // 4

    def conv_rows(r_lo, r_hi, after):
        for r0 in range(r_lo, r_hi, ROW_CHUNK):
            acc = jnp.broadcast_to(conv_b, (ROW_CHUNK, D_GROUP))
            if r0 == r_lo:
                acc = acc + jnp.tile(_runtime_zero(after), (ROW_CHUNK // SUBLANES, D_GROUP // LANES))
            for tap in range(CONV_WIDTH):
                off = HALO + r0 - (CONV_WIDTH - 1) + tap
                base, s = off - off % SUBLANES, off % SUBLANES
                if s == 0:
                    rows = glu_buf[base:base + ROW_CHUNK, :]
                else:
                    assert base + ROW_CHUNK <= n_shift
                    rows = shift_buf[s - 1, base:base + ROW_CHUNK, :]
                acc = acc + conv_w_ref[tap:tap + 1, :] * rows
            conv_buf[r0:r0 + ROW_CHUNK, :] = acc
        return first_vreg(acc)

    za = proj(O_A, O_Q)
    glu_buf[HALO:n, :] = za[:, :D_GROUP] * _sigmoid(za[:, D_GROUP:])
    q_ref[0] = (proj(O_Q, O_K) * Q_SCALE).astype(BF16)
    for s in range(1, SUBLANES):
        shift_buf[s - 1] = glu_buf[s:s + n_shift, :]

    zk = proj(O_K, O_V)
    k_ref[0] = zk.astype(BF16)
    c0 = conv_rows(0, quarter, first_vreg(zk))
    zv = proj(O_V, O_U)
    v_ref[0] = zv.astype(BF16)
    c1 = conv_rows(quarter, 2 * quarter, first_vreg(zv))
    zu = proj(O_U, O_SV, after=c0)
    gu_buf[...] = zu
    c2 = conv_rows(2 * quarter, 3 * quarter, first_vreg(zu))
    zsv = proj(O_SV, O_P, after=c1)
    gv_buf[...] = zsv
    conv_rows(3 * quarter, tm, first_vreg(zsv))
    pool_buf[HALO:n, :] = proj(O_P, D_IN, after=c2)

    hc = _layer_norm(conv_buf[...], cln_g, cln_b)
    hc = hc * _sigmoid(hc)
    y_conv = jnp.dot(hc.astype(BF16), pw_w_ref[...], preferred_element_type=F32) + pw_b
    ymix_ref[0, :, 0:D_GROUP] = y_conv.astype(BF16)

    gu = _gelu_tanh(gu_buf[...])
    gv = _layer_norm(_gelu_tanh(gv_buf[...]), sgln_g, sgln_b).astype(BF16)
    lane = lax.broadcasted_iota(jnp.int32, (CHUNK, D_GROUP), 1)
    sg_w = sg_w_ref[...]
    sg_bias = sg_bias_ref[...]
    for c in range(tm // CHUNK):
        vc = gv[c * CHUNK:(c + 1) * CHUNK, :]
        stacked = jnp.concatenate(
            [jnp.where((lane >= hd * SG_HEAD_DIM) & (lane < (hd + 1) * SG_HEAD_DIM), vc,
                       jnp.zeros_like(vc)) for hd in range(N_SG_HEADS)], axis=0)
        mixed = jnp.dot(sg_w, stacked, preferred_element_type=F32) + sg_bias
        ymix_ref[0, c * CHUNK:(c + 1) * CHUNK, D_GROUP:2 * D_GROUP] = (
            gu[c * CHUNK:(c + 1) * CHUNK, :] * mixed).astype(BF16)

    assert POOL_WINDOWS == (2, 4, 8, 16) and HALO == 32
    win_buf[0, 8:n, :] = pool_buf[8:n, :] + pool_buf[7:n - 1, :]
    win_buf[1, 16:n, :] = win_buf[0, 16:n, :] + win_buf[0, 14:n - 2, :]
    win_buf[2, 24:n, :] = win_buf[1, 24:n, :] + win_buf[1, 20:n - 4, :]
    sum16 = win_buf[2, HALO:n, :] + win_buf[2, HALO - 8:n - 8, :]
    lane_r = lax.broadcasted_iota(jnp.int32, (tm, D_GROUP), 1)
    row_r = lax.broadcasted_iota(jnp.int32, (tm, D_GROUP), 0)
    group = lane_r // POOL_GROUP_DIM
    window = jnp.where(group == 0, POOL_WINDOWS[0],
                       jnp.where(group == 1, POOL_WINDOWS[1],
                                 jnp.where(group == 2, POOL_WINDOWS[2], POOL_WINDOWS[3])))
    total = jnp.where(group == 0, win_buf[0, HALO:n, :],
                      jnp.where(group == 1, win_buf[1, HALO:n, :],
                                jnp.where(group == 2, win_buf[2, HALO:n, :], sum16)))
    cnt = jnp.minimum(j * tm + row_r + 1, window).astype(F32)
    pooled = total / cnt - pool_buf[HALO:n, :]
    y_pool = jnp.dot(pooled.astype(BF16), pool_w_ref[...], preferred_element_type=F32)
    ymix_ref[0, :, 2 * D_GROUP:3 * D_GROUP] = (y_pool * pool_scale).astype(BF16)


def _const_spec(shape):
    return pl.BlockSpec(shape, lambda *_: (0,) * len(shape))


def _layer_spec(stacked, layer, **kwargs):
    zeros = (0,) * (stacked.ndim - 1)
    return pl.BlockSpec((None,) + stacked.shape[1:], lambda *_: (layer,) + zeros, **kwargs)


def _mixer_in(layer, x, emb_g, emb_b, stacked, vectors):
    first = layer == 0
    B, S, _ = x.shape
    tm = TM_IN
    tok = lambda width: pl.BlockSpec((1, tm, width), lambda b, j: (b, j, 0))
    out_shape = [jax.ShapeDtypeStruct((B, S, D_GROUP), BF16),
                 jax.ShapeDtypeStruct((B, S, D_GROUP), BF16),
                 jax.ShapeDtypeStruct((B, S, D_GROUP), BF16),
                 jax.ShapeDtypeStruct((B, S, 3 * D_GROUP), BF16)]
    out_specs = [tok(D_GROUP), tok(D_GROUP), tok(D_GROUP), tok(3 * D_GROUP)]
    if first:
        out_shape = [jax.ShapeDtypeStruct((B, S, D_MODEL), F32)] + out_shape
        out_specs = [tok(D_MODEL)] + out_specs
    return pl.pallas_call(
        functools.partial(_mixer_in_kernel, layer),
        grid=(B, S // tm),
        in_specs=([tok(D_MODEL), _const_spec(emb_g.shape), _const_spec(emb_b.shape)]
                  + [_layer_spec(p, layer) for p in stacked] + [_const_spec(p.shape) for p in vectors]),
        out_specs=out_specs,
        out_shape=out_shape,
        scratch_shapes=[pltpu.VMEM((HALO + tm, D_GROUP), F32),
                        pltpu.VMEM((SUBLANES - 1, HALO + tm - SUBLANES, D_GROUP), F32),
                        pltpu.VMEM((HALO + tm, D_GROUP), F32),
                        pltpu.VMEM((3, HALO + tm, D_GROUP), F32),
                        pltpu.VMEM((tm, D_GROUP), F32),
                        pltpu.VMEM((tm, D_GROUP), F32),
                        pltpu.VMEM((tm, D_GROUP), F32)],
        compiler_params=pltpu.CompilerParams(dimension_semantics=("parallel", "arbitrary"),
                                             vmem_limit_bytes=VMEM_LIMIT_BYTES),
        name="mixer_in",
    )(x, emb_g, emb_b, *stacked, *vectors)


def _diff_attn_kernel(layer, q_ref, k_ref, v_ref, table_ref, lq1_ref, lk1_ref, lq2_ref, lk2_ref, g_ref,
                      o_ref, bias_ref, vt_ref, qm_ref, s_ref, m_ref, a_ref, acc_ref):
    lam_init = 0.8 - 0.6 * math.exp(-0.3 * layer)
    i = pl.program_id(1)
    n_kv = k_ref.shape[1] // TK

    @pl.when((pl.program_id(0) == 0) & (i == 0))
    def _():
        for hd in range(N_HEADS):
            spread = jnp.broadcast_to(table_ref[hd:hd + 1, :], (TK, table_ref.shape[1]))
            sheared = pltpu.roll(spread, 0, axis=1, stride=1, stride_axis=0)
            bias_ref[0, :, hd * TQ:(hd + 1) * TQ] = sheared[:, 0:TQ]
            bias_ref[1, :, hd * TQ:(hd + 1) * TQ] = sheared[:, TK:TK + TQ]

    @pl.when(i == 0)
    def _():
        row = lax.broadcasted_iota(jnp.int32, (V_AUG - V_DIM, TK), 0)
        ones_rows = jnp.where(row == 0, 1.0, 0.0).astype(BF16)
        for jt in range(n_kv):
            vt = v_ref[0, jt * TK:(jt + 1) * TK, :].astype(F32).T
            for hd in range(N_HEADS):
                vt_ref[jt, hd, 0:V_DIM, :] = vt[hd * V_DIM:(hd + 1) * V_DIM, :].astype(BF16)
                vt_ref[jt, hd, V_DIM:V_AUG, :] = ones_rows

    q = q_ref[0].astype(F32)
    row = lax.broadcasted_iota(jnp.int32, (LANES, TQ), 0)
    for mp in range(2):
        qt = q[:, mp * LANES:(mp + 1) * LANES].T
        for hd in range(N_HEADS):
            keep = (row >= hd * QK_DIM) & (row < (hd + 1) * QK_DIM)
            qm_ref[hd * 2 + mp] = jnp.where(keep, qt, 0.0).astype(BF16)

    m_ref[1] = jnp.full(m_ref.shape[1:], MASK_VALUE, F32)
    acc_ref[...] = jnp.zeros(acc_ref.shape, F32)

    def scores_into(slot, t, bias, hd):
        rows = pl.ds(pl.multiple_of(t * TK, TK), TK)
        for mp in range(2):
            blk = hd * 2 + mp
            cols = slice(blk * TQ, (blk + 1) * TQ)
            st = jnp.dot(k_ref[0, rows, mp * LANES:(mp + 1) * LANES], qm_ref[blk],
                         preferred_element_type=F32)
            if bias is not None:
                st = st + bias_ref[bias, :, hd * TQ:(hd + 1) * TQ]
            s_ref[slot, :, cols] = st
            m_old = m_ref[1 - slot, :, cols]
            m_new = jnp.maximum(m_old, jnp.max(st, axis=0, keepdims=True))
            m_ref[slot, :, cols] = m_new
            a_ref[slot, :, cols] = jnp.exp2(m_old - m_new)

    def values_from(slot, t, hd):
        cols = slice(hd * 2 * TQ, (hd + 1) * 2 * TQ)
        p = jnp.exp2(s_ref[slot, :, cols] - m_ref[slot, :, cols]).astype(BF16)
        pv = jnp.dot(vt_ref[t, hd], p, preferred_element_type=F32)
        acc_ref[hd] = acc_ref[hd] * a_ref[slot, :, cols] + pv

    def stage(score_args, value_args):
        for hd in range(N_HEADS):
            if score_args is not None:
                scores_into(*score_args, hd)
            if value_args is not None:
                values_from(*value_args, hd)

    def last_values_and_finalize(slot):
        row = slice(layer, layer + 1)
        lam = (jnp.exp(jnp.sum(lq1_ref[row, :] * lk1_ref[row, :], keepdims=True))
               - jnp.exp(jnp.sum(lq2_ref[row, :] * lk2_ref[row, :], keepdims=True)) + lam_init)
        outs = []
        for hd in range(N_HEADS):
            values_from(slot, 0, hd)
            a = acc_ref[hd]
            o1 = a[0:V_DIM, 0:TQ] / a[V_DIM:V_DIM + 1, 0:TQ]
            o2 = a[0:V_DIM, TQ:2 * TQ] / a[V_DIM:V_DIM + 1, TQ:2 * TQ]
            o = o1 - lam * o2
            ms = jnp.mean(o * o, axis=0, keepdims=True)
            outs.append(o * lax.rsqrt(ms + LN_EPS) * (1.0 - lam_init))
        o_ref[0] = (jnp.concatenate(outs, axis=0).T * g_ref[row, :]).astype(BF16)

    @pl.when(i == 0)
    def _():
        stage((0, 0, 0), None)
        last_values_and_finalize(0)

    @pl.when(i >= 1)
    def _():
        stage((0, i, 0), None)
        stage((1, i - 1, 1), (0, i))

    def far_pair(k2, carry):
        t = i - 2 - 2 * k2
        stage((0, t, None), (1, t + 1))
        stage((1, t - 1, None), (0, t))
        return carry

    n_far = jnp.maximum(i - 1, 0)
    lax.fori_loop(0, n_far // 2, far_pair, 0)

    @pl.when((i >= 2) & (i % 2 == 0))
    def _():
        stage((0, 0, None), (1, 1))
        last_values_and_finalize(0)

    @pl.when(i % 2 == 1)
    def _():
        last_values_and_finalize(1)


def _diff_attn(layer, q, k, v, table, vectors):
    B, S, _ = q.shape
    n_kv = S // TK
    return pl.pallas_call(
        functools.partial(_diff_attn_kernel, layer),
        grid=(B, S // TQ),
        in_specs=[pl.BlockSpec((1, TQ, D_GROUP), lambda b, i: (b, i, 0)),
                  pl.BlockSpec((1, S, D_GROUP), lambda b, i: (b, 0, 0)),
                  pl.BlockSpec((1, S, D_GROUP), lambda b, i: (b, 0, 0)),
                  _const_spec(table.shape)] + [_const_spec(p.shape) for p in vectors],
        out_specs=pl.BlockSpec((1, TQ, D_GROUP), lambda b, i: (b, i, 0)),
        out_shape=jax.ShapeDtypeStruct((B, S, D_GROUP), BF16),
        scratch_shapes=[pltpu.VMEM((2, TK, N_HEADS * TQ), F32),
                        pltpu.VMEM((n_kv, N_HEADS, V_AUG, TK), BF16),
                        pltpu.VMEM((N_MAPS, LANES, TQ), BF16),
                        pltpu.VMEM((2, TK, N_MAPS * TQ), F32),
                        pltpu.VMEM((2, 1, N_MAPS * TQ), F32),
                        pltpu.VMEM((2, 1, N_MAPS * TQ), F32),
                        pltpu.VMEM((N_HEADS, V_AUG, 2 * TQ), F32)],
        compiler_params=pltpu.CompilerParams(dimension_semantics=("arbitrary", "arbitrary"),
                                             vmem_limit_bytes=VMEM_LIMIT_BYTES),
        name="diff_attn",
    )(q, k, v, table, *vectors)


def _mixer_out_kernel(layer, h_ref, ymix_ref, ydiff_ref, w_out_ref, w1_ref, w2_ref, ln1_g_ref, ln1_b_ref,
                      ln2_g_ref, ln2_b_ref, o_ref):
    ln1_g, ln1_b, ln2_g, ln2_b = (r[layer:layer + 1, :] for r in (ln1_g_ref, ln1_b_ref, ln2_g_ref, ln2_b_ref))
    tm = h_ref.shape[0]
    halves = [slice(r, r + OUT_ROWS) for r in range(0, tm, OUT_ROWS)]
    mix = []
    for rows in halves:
        ymix = ymix_ref[rows, :]
        mix.append(jnp.dot(ymix[:, 0:D_GROUP], w_out_ref[0:D_GROUP, :], preferred_element_type=F32)
                   + jnp.dot(ydiff_ref[rows, :], w_out_ref[D_GROUP:2 * D_GROUP, :],
                             preferred_element_type=F32)
                   + jnp.dot(ymix[:, D_GROUP:3 * D_GROUP], w_out_ref[2 * D_GROUP:4 * D_GROUP, :],
                             preferred_element_type=F32))
    h1 = [_layer_norm(ALPHA * h_ref[rows, :] + m, ln1_g, ln1_b) for rows, m in zip(halves, mix)]
    act = []
    for x in h1:
        a = jnp.maximum(jnp.dot(x.astype(BF16), w1_ref[...], preferred_element_type=F32), 0.0)
        act.append((a * a).astype(BF16))
    ff = [jnp.dot(a, w2_ref[...], preferred_element_type=F32) for a in act]
    for rows, x, f in zip(halves, h1, ff):
        o_ref[rows, :] = _layer_norm(ALPHA * x + f, ln2_g, ln2_b)


def _mixer_out(layer, h, ymix, ydiff, stacked, vectors):
    T = h.shape[0]
    tm = TM_OUT
    tok = lambda width: pl.BlockSpec((tm, width), lambda t: (t, 0))
    return pl.pallas_call(
        functools.partial(_mixer_out_kernel, layer),
        grid=(T // tm,),
        in_specs=([tok(D_MODEL), tok(3 * D_GROUP), tok(D_GROUP)]
                  + [_layer_spec(p, layer, pipeline_mode=pl.Buffered(1)) for p in stacked]
                  + [_const_spec(p.shape) for p in vectors]),
        out_specs=tok(D_MODEL),
        out_shape=jax.ShapeDtypeStruct((T, D_MODEL), F32),
        compiler_params=pltpu.CompilerParams(dimension_semantics=("parallel",),
                                             vmem_limit_bytes=VMEM_LIMIT_BYTES),
        name="mixer_out",
    )(h, ymix, ydiff, *stacked, *vectors)


def _t5_bucket(n):
    max_exact = N_BUCKETS // 2
    large = max_exact + (jnp.log(jnp.maximum(n, 1).astype(F32) / max_exact)
                         / math.log(MAX_DISTANCE / max_exact) * (N_BUCKETS - max_exact)).astype(jnp.int32)
    large = jnp.minimum(large, N_BUCKETS - 1)
    return jnp.where(n < max_exact, n, large)


def _bias_table(rel_bias):
    assert TK >= MAX_DISTANCE, "tiles two or more behind the diagonal must see one constant bias"
    dist = jnp.arange(TQ + TK)
    onehot = (_t5_bucket(dist)[:, None] == jnp.arange(N_BUCKETS)[None, :]).astype(F32)
    rb = rel_bias.astype(F32)
    vals = jnp.sum(onehot[:, :, None] * (rb - rb[N_BUCKETS - 1])[None, :, :], axis=1) * LOG2E
    return jnp.concatenate([vals, jnp.full((TQ + TK, N_HEADS), MASK_VALUE, F32)], axis=0).T


def _block_diag(w):
    n_layers, g, c, d = w.shape
    out = jnp.zeros((n_layers, g * c, g * d), w.dtype)
    for i in range(g):
        out = out.at[:, i * c:(i + 1) * c, i * d:(i + 1) * d].set(w[:, i])
    return out


def kernel(x, emb_ln_g, emb_ln_b, rel_bias, w_in, conv_w, conv_b, conv_ln_g, conv_ln_b, conv_pw_w,
           conv_pw_b, lam_q1, lam_k1, lam_q2, lam_k2, diff_norm_g, sg_ln_g, sg_ln_b, sg_w, sg_b, pool_w,
           pool_scale, w_out, ln1_g, ln1_b, w_mlp1, w_mlp2, ln2_g, ln2_b):
    B, S, _ = x.shape
    assert S % TM_IN == 0 and (B * S) % TM_OUT == 0 and S % TQ == 0 and TQ == TK
    small = [conv_b, conv_ln_g, conv_ln_b, conv_pw_b, sg_ln_g, sg_ln_b, pool_scale, lam_q1, lam_k1,
             lam_q2, lam_k2, diff_norm_g, ln1_g, ln1_b, ln2_g, ln2_b]
    assert all(p.dtype == F32 and p.shape[0] == DEPTH and p.ndim == 2 for p in small)
    tril = jnp.tril(jnp.ones((CHUNK, CHUNK), F32))
    sg_w_cat = jnp.transpose(sg_w * tril, (0, 2, 1, 3)).reshape(DEPTH, CHUNK, N_SG_HEADS * CHUNK)
    sg_bias = jnp.repeat(jnp.swapaxes(sg_b, 1, 2), SG_HEAD_DIM, axis=2).astype(F32)
    in_matrices = [w_in.astype(BF16), conv_w.astype(F32), conv_pw_w.astype(BF16), sg_w_cat.astype(BF16),
                   sg_bias, _block_diag(pool_w).astype(BF16)]
    in_vectors = [conv_b, conv_ln_g, conv_ln_b, conv_pw_b, sg_ln_g, sg_ln_b, pool_scale]
    attn_vectors = [lam_q1, lam_k1, lam_q2, lam_k2, jnp.tile(diff_norm_g, (1, N_HEADS))]
    out_matrices = [w_out.astype(BF16), w_mlp1.astype(BF16), w_mlp2.astype(BF16)]
    out_vectors = [ln1_g, ln1_b, ln2_g, ln2_b]
    table = _bias_table(rel_bias)
    emb_g, emb_b = emb_ln_g.reshape(1, -1).astype(F32), emb_ln_b.reshape(1, -1).astype(F32)

    h = x
    for l in range(DEPTH):
        outs = _mixer_in(l, h, emb_g, emb_b, in_matrices, in_vectors)
        if l == 0:
            h, q, k, v, ymix = outs
        else:
            q, k, v, ymix = outs
        ydiff = _diff_attn(l, q, k, v, table, attn_vectors)
        h = _mixer_out(l, h.reshape(B * S, D_MODEL), ymix.reshape(B * S, 3 * D_GROUP),
                       ydiff.reshape(B * S, D_GROUP), out_matrices, out_vectors).reshape(B, S, D_MODEL)
    return h
```

```python
import functools
import math

import jax
import jax.numpy as jnp
import numpy as np
from jax import lax
from jax.experimental import pallas as pl
from jax.experimental.pallas import tpu as pltpu

F32 = jnp.float32
BF16 = jnp.bfloat16

D_MODEL = 1024
DEPTH = 2
D_GROUP = 256
CONV_WIDTH = 31
N_HEADS = 4
QK_DIM = 32
V_DIM = 64
N_SG_HEADS = 4
SG_HEAD_DIM = 64
CHUNK = 128
POOL_WINDOWS = (2, 4, 8, 16)
POOL_GROUP_DIM = 64
D_FF = 4096
N_BUCKETS = 32
MAX_DISTANCE = 128
LN_EPS = 1e-5
ALPHA = (2.0 * DEPTH) ** 0.25
D_IN = 2048
O_A, O_Q, O_K, O_V, O_U, O_SV, O_P = 0, 512, 768, 1024, 1280, 1536, 1792

LANES = 128
SUBLANES = 8
VMEM_LIMIT_BYTES = 56 * 1024 * 1024

TM_IN = 1024
TM_OUT = 512
OUT_ROWS = 256
WEIGHT_CHUNK = 1024
TQ = 256
TK = 256
HALO = 32
ROW_CHUNK = 64
V_AUG = 80
N_MAPS = 2 * N_HEADS
MASK_VALUE = -1e30
LOG2E = math.log2(math.e)
Q_SCALE = (QK_DIM ** -0.5) * LOG2E


def _layer_norm(x, g, b):
    mu = jnp.mean(x, axis=-1, keepdims=True)
    xc = x - mu
    var = jnp.mean(xc * xc, axis=-1, keepdims=True)
    return xc * lax.rsqrt(var + LN_EPS) * g + b


def _gelu_tanh(x):
    c = math.sqrt(2.0 / math.pi)
    return 0.5 * x * (1.0 + jnp.tanh(c * (x + 0.044715 * (x * x * x))))


def _sigmoid(x):
    return 1.0 / (1.0 + jnp.exp(-x))


def _runtime_zero_bits(v):
    bits = pltpu.bitcast(v, jnp.int32)
    return lax.shift_right_logical(lax.shift_right_logical(bits, 16), 16)


def _runtime_zero(v):
    return _runtime_zero_bits(v).astype(F32)


def _mixer_in_kernel(layer, x_ref, emb_g_ref, emb_b_ref, w_in_hbm, conv_w_ref, pw_w_ref, sg_w_ref,
                     sg_bias_ref, pool_w_ref, *rest):
    first = layer == 0
    conv_b, cln_g, cln_b, pw_b, sgln_g, sgln_b, pool_scale = (r[layer:layer + 1, :] for r in rest[:7])
    rest = rest[7:]
    if first:
        h_ref, *rest = rest
    (q_ref, k_ref, v_ref, ymix_ref, w_in_ref, stage_ref, sem, glu_buf, shift_buf, pool_buf, win_buf,
     conv_buf, gu_buf, gv_buf) = rest
    tm = x_ref.shape[1]
    j = pl.program_id(1)

    @pl.when((pl.program_id(0) == 0) & (j == 0))
    def _():
        side = stage_ref.shape[1]
        cols = lambda c: pl.ds(c * side, side)
        _stream_cast([(w_in_hbm.at[layer, :, cols(c)], w_in_ref.at[:, cols(c)])
                      for c in range(D_IN // side)], stage_ref, sem)

    @pl.when(j == 0)
    def _():
        glu_buf[0:HALO, :] = jnp.zeros((HALO, D_GROUP), F32)
        pool_buf[0:HALO, :] = jnp.zeros((HALO, D_GROUP), F32)

    @pl.when(j > 0)
    def _():
        glu_buf[0:HALO, :] = glu_buf[tm:tm + HALO, :]
        pool_buf[0:HALO, :] = pool_buf[tm:tm + HALO, :]

    x = x_ref[0]
    if first:
        h = _layer_norm(x, emb_g_ref[...], emb_b_ref[...])
        h_ref[0] = h
    else:
        h = x
    hb = h.astype(BF16)

    def first_vreg(value):
        return value[0:SUBLANES, 0:LANES]

    def proj(lo, hi, after=None):
        w = w_in_ref[:, lo:hi]
        if after is not None:
            bits = pltpu.bitcast(w, jnp.int32)
            zero = _runtime_zero_bits(after)
            bits = bits | jnp.tile(zero, (bits.shape[0] // SUBLANES, bits.shape[1] // LANES))
            w = pltpu.bitcast(bits, BF16)
        return jnp.dot(hb, w, preferred_element_type=F32)

    n = HALO + tm
    n_shift = n - SUBLANES
    quarter = tm // 4

    def conv_rows(r_lo, r_hi, after):
        for r0 in range(r_lo, r_hi, ROW_CHUNK):
            acc = jnp.broadcast_to(conv_b, (ROW_CHUNK, D_GROUP))
            if r0 == r_lo:
                acc = acc + jnp.tile(_runtime_zero(after), (ROW_CHUNK // SUBLANES, D_GROUP // LANES))
            for tap in range(CONV_WIDTH):
                off = HALO + r0 - (CONV_WIDTH - 1) + tap
                base, s = off - off % SUBLANES, off % SUBLANES
                if s == 0:
                    rows = glu_buf[base:base + ROW_CHUNK, :]
                else:
                    assert base + ROW_CHUNK <= n_shift
                    rows = shift_buf[s - 1, base:base + ROW_CHUNK, :]
                acc = acc + conv_w_ref[tap:tap + 1, :] * rows
            conv_buf[r0:r0 + ROW_CHUNK, :] = acc
        return first_vreg(acc)

    za = proj(O_A, O_Q)
    glu_buf[HALO:n, :] = za[:, :D_GROUP] * _sigmoid(za[:, D_GROUP:])
    q_ref[0] = (proj(O_Q, O_K) * Q_SCALE).astype(BF16)
    for s in range(1, SUBLANES):
        shift_buf[s - 1] = glu_buf[s:s + n_shift, :]

    zk = proj(O_K, O_V)
    k_ref[0] = zk.astype(BF16)
    c0 = conv_rows(0, quarter, first_vreg(zk))
    zv = proj(O_V, O_U)
    v_ref[0] = zv.astype(BF16)
    c1 = conv_rows(quarter, 2 * quarter, first_vreg(zv))
    zu = proj(O_U, O_SV, after=c0)
    gu_buf[...] = zu
    c2 = conv_rows(2 * quarter, 3 * quarter, first_vreg(zu))
    zsv = proj(O_SV, O_P, after=c1)
    gv_buf[...] = zsv
    conv_rows(3 * quarter, tm, first_vreg(zsv))
    pool_buf[HALO:n, :] = proj(O_P, D_IN, after=c2)

    hc = _layer_norm(conv_buf[...], cln_g, cln_b)
    hc = hc * _sigmoid(hc)
    y_conv = jnp.dot(hc.astype(BF16), pw_w_ref[...], preferred_element_type=F32) + pw_b
    ymix_ref[0, :, 0:D_GROUP] = y_conv.astype(BF16)

    gu = _gelu_tanh(gu_buf[...])
    gv = _layer_norm(_gelu_tanh(gv_buf[...]), sgln_g, sgln_b).astype(BF16)
    lane = lax.broadcasted_iota(jnp.int32, (CHUNK, D_GROUP), 1)
    sg_w = sg_w_ref[...]
    sg_bias = sg_bias_ref[...]
    for c in range(tm // CHUNK):
        vc = gv[c * CHUNK:(c + 1) * CHUNK, :]
        stacked = jnp.concatenate(
            [jnp.where((lane >= hd * SG_HEAD_DIM) & (lane < (hd + 1) * SG_HEAD_DIM), vc,
                       jnp.zeros_like(vc)) for hd in range(N_SG_HEADS)], axis=0)
        mixed = jnp.dot(sg_w, stacked, preferred_element_type=F32) + sg_bias
        ymix_ref[0, c * CHUNK:(c + 1) * CHUNK, D_GROUP:2 * D_GROUP] = (
            gu[c * CHUNK:(c + 1) * CHUNK, :] * mixed).astype(BF16)

    assert POOL_WINDOWS == (2, 4, 8, 16) and HALO == 32
    win_buf[0, 8:n, :] = pool_buf[8:n, :] + pool_buf[7:n - 1, :]
    win_buf[1, 16:n, :] = win_buf[0, 16:n, :] + win_buf[0, 14:n - 2, :]
    win_buf[2, 24:n, :] = win_buf[1, 24:n, :] + win_buf[1, 20:n - 4, :]
    sum16 = win_buf[2, HALO:n, :] + win_buf[2, HALO - 8:n - 8, :]
    lane_r = lax.broadcasted_iota(jnp.int32, (tm, D_GROUP), 1)
    row_r = lax.broadcasted_iota(jnp.int32, (tm, D_GROUP), 0)
    group = lane_r // POOL_GROUP_DIM
    window = jnp.where(group == 0, POOL_WINDOWS[0],
                       jnp.where(group == 1, POOL_WINDOWS[1],
                                 jnp.where(group == 2, POOL_WINDOWS[2], POOL_WINDOWS[3])))
    total = jnp.where(group == 0, win_buf[0, HALO:n, :],
                      jnp.where(group == 1, win_buf[1, HALO:n, :],
                                jnp.where(group == 2, win_buf[2, HALO:n, :], sum16)))
    cnt = jnp.minimum(j * tm + row_r + 1, window).astype(F32)
    pooled = total / cnt - pool_buf[HALO:n, :]
    y_pool = jnp.dot(pooled.astype(BF16), pool_w_ref[...], preferred_element_type=F32)
    ymix_ref[0, :, 2 * D_GROUP:3 * D_GROUP] = (y_pool * pool_scale).astype(BF16)


def _const_spec(shape):
    return pl.BlockSpec(shape, lambda *_: (0,) * len(shape))


def _layer_spec(stacked, layer, **kwargs):
    zeros = (0,) * (stacked.ndim - 1)
    return pl.BlockSpec((None,) + stacked.shape[1:], lambda *_: (layer,) + zeros, **kwargs)


def _mixer_in(layer, x, emb_g, emb_b, w_in, stacked, vectors):
    first = layer == 0
    assert D_MODEL == WEIGHT_CHUNK and D_IN % WEIGHT_CHUNK == 0
    B, S, _ = x.shape
    tm = TM_IN
    tok = lambda width: pl.BlockSpec((1, tm, width), lambda b, j: (b, j, 0))
    out_shape = [jax.ShapeDtypeStruct((B, S, D_GROUP), BF16),
                 jax.ShapeDtypeStruct((B, S, D_GROUP), BF16),
                 jax.ShapeDtypeStruct((B, S, D_GROUP), BF16),
                 jax.ShapeDtypeStruct((B, S, 3 * D_GROUP), BF16)]
    out_specs = [tok(D_GROUP), tok(D_GROUP), tok(D_GROUP), tok(3 * D_GROUP)]
    if first:
        out_shape = [jax.ShapeDtypeStruct((B, S, D_MODEL), F32)] + out_shape
        out_specs = [tok(D_MODEL)] + out_specs
    return pl.pallas_call(
        functools.partial(_mixer_in_kernel, layer),
        grid=(B, S // tm),
        in_specs=([tok(D_MODEL), _const_spec(emb_g.shape), _const_spec(emb_b.shape),
                   pl.BlockSpec(memory_space=pl.ANY)]
                  + [_layer_spec(p, layer) for p in stacked] + [_const_spec(p.shape) for p in vectors]),
        out_specs=out_specs,
        out_shape=out_shape,
        scratch_shapes=[pltpu.VMEM((D_MODEL, D_IN), BF16),
                        pltpu.VMEM((2, WEIGHT_CHUNK, WEIGHT_CHUNK), F32),
                        pltpu.SemaphoreType.DMA((2,)),
                        pltpu.VMEM((HALO + tm, D_GROUP), F32),
                        pltpu.VMEM((SUBLANES - 1, HALO + tm - SUBLANES, D_GROUP), F32),
                        pltpu.VMEM((HALO + tm, D_GROUP), F32),
                        pltpu.VMEM((3, HALO + tm, D_GROUP), F32),
                        pltpu.VMEM((tm, D_GROUP), F32),
                        pltpu.VMEM((tm, D_GROUP), F32),
                        pltpu.VMEM((tm, D_GROUP), F32)],
        compiler_params=pltpu.CompilerParams(dimension_semantics=("arbitrary", "arbitrary"),
                                             vmem_limit_bytes=VMEM_LIMIT_BYTES),
        name="mixer_in",
    )(x, emb_g, emb_b, w_in, *stacked, *vectors)


def _diff_attn_kernel(layer, q_ref, k_ref, v_ref, table_ref, lq1_ref, lk1_ref, lq2_ref, lk2_ref, g_ref,
                      o_ref, bias_ref, vt_ref, qm_ref, s_ref, m_ref, a_ref, acc_ref):
    lam_init = 0.8 - 0.6 * math.exp(-0.3 * layer)
    i = pl.program_id(1)
    n_kv = k_ref.shape[1] // TK

    @pl.when((pl.program_id(0) == 0) & (i == 0))
    def _():
        for hd in range(N_HEADS):
            spread = jnp.broadcast_to(table_ref[hd:hd + 1, :], (TK, table_ref.shape[1]))
            sheared = pltpu.roll(spread, 0, axis=1, stride=1, stride_axis=0)
            bias_ref[0, :, hd * TQ:(hd + 1) * TQ] = sheared[:, 0:TQ]
            bias_ref[1, :, hd * TQ:(hd + 1) * TQ] = sheared[:, TK:TK + TQ]

    @pl.when(i == 0)
    def _():
        row = lax.broadcasted_iota(jnp.int32, (V_AUG - V_DIM, TK), 0)
        ones_rows = jnp.where(row == 0, 1.0, 0.0).astype(BF16)
        for jt in range(n_kv):
            vt = v_ref[0, jt * TK:(jt + 1) * TK, :].astype(F32).T
            for hd in range(N_HEADS):
                vt_ref[jt, hd, 0:V_DIM, :] = vt[hd * V_DIM:(hd + 1) * V_DIM, :].astype(BF16)
                vt_ref[jt, hd, V_DIM:V_AUG, :] = ones_rows

    q = q_ref[0].astype(F32)
    row = lax.broadcasted_iota(jnp.int32, (LANES, TQ), 0)
    for mp in range(2):
        qt = q[:, mp * LANES:(mp + 1) * LANES].T
        for hd in range(N_HEADS):
            keep = (row >= hd * QK_DIM) & (row < (hd + 1) * QK_DIM)
            qm_ref[hd * 2 + mp] = jnp.where(keep, qt, 0.0).astype(BF16)

    m_ref[1] = jnp.full(m_ref.shape[1:], MASK_VALUE, F32)
    acc_ref[...] = jnp.zeros(acc_ref.shape, F32)

    def scores_into(slot, t, bias, hd):
        rows = pl.ds(pl.multiple_of(t * TK, TK), TK)
        for mp in range(2):
            blk = hd * 2 + mp
            cols = slice(blk * TQ, (blk + 1) * TQ)
            st = jnp.dot(k_ref[0, rows, mp * LANES:(mp + 1) * LANES], qm_ref[blk],
                         preferred_element_type=F32)
            if bias is not None:
                st = st + bias_ref[bias, :, hd * TQ:(hd + 1) * TQ]
            s_ref[slot, :, cols] = st
            m_old = m_ref[1 - slot, :, cols]
            m_new = jnp.maximum(m_old, jnp.max(st, axis=0, keepdims=True))
            m_ref[slot, :, cols] = m_new
            a_ref[slot, :, cols] = jnp.exp2(m_old - m_new)

    def values_from(slot, t, hd):
        cols = slice(hd * 2 * TQ, (hd + 1) * 2 * TQ)
        p = jnp.exp2(s_ref[slot, :, cols] - m_ref[slot, :, cols]).astype(BF16)
        pv = jnp.dot(vt_ref[t, hd], p, preferred_element_type=F32)
        acc_ref[hd] = acc_ref[hd] * a_ref[slot, :, cols] + pv

    def stage(score_args, value_args):
        for hd in range(N_HEADS):
            if score_args is not None:
                scores_into(*score_args, hd)
            if value_args is not None:
                values_from(*value_args, hd)

    def last_values_and_finalize(slot):
        row = slice(layer, layer + 1)
        lam = (jnp.exp(jnp.sum(lq1_ref[row, :] * lk1_ref[row, :], keepdims=True))
               - jnp.exp(jnp.sum(lq2_ref[row, :] * lk2_ref[row, :], keepdims=True)) + lam_init)
        outs = []
        for hd in range(N_HEADS):
            values_from(slot, 0, hd)
            a = acc_ref[hd]
            o1 = a[0:V_DIM, 0:TQ] / a[V_DIM:V_DIM + 1, 0:TQ]
            o2 = a[0:V_DIM, TQ:2 * TQ] / a[V_DIM:V_DIM + 1, TQ:2 * TQ]
            o = o1 - lam * o2
            ms = jnp.mean(o * o, axis=0, keepdims=True)
            outs.append(o * lax.rsqrt(ms + LN_EPS) * (1.0 - lam_init))
        o_ref[0] = (jnp.concatenate(outs, axis=0).T * g_ref[row, :]).astype(BF16)

    @pl.when(i == 0)
    def _():
        stage((0, 0, 0), None)
        last_values_and_finalize(0)

    @pl.when(i >= 1)
    def _():
        stage((0, i, 0), None)
        stage((1, i - 1, 1), (0, i))

    def far_pair(k2, carry):
        t = i - 2 - 2 * k2
        stage((0, t, None), (1, t + 1))
        stage((1, t - 1, None), (0, t))
        return carry

    n_far = jnp.maximum(i - 1, 0)
    lax.fori_loop(0, n_far // 2, far_pair, 0)

    @pl.when((i >= 2) & (i % 2 == 0))
    def _():
        stage((0, 0, None), (1, 1))
        last_values_and_finalize(0)

    @pl.when(i % 2 == 1)
    def _():
        last_values_and_finalize(1)


def _diff_attn(layer, q, k, v, table, vectors):
    B, S, _ = q.shape
    n_kv = S // TK
    return pl.pallas_call(
        functools.partial(_diff_attn_kernel, layer),
        grid=(B, S // TQ),
        in_specs=[pl.BlockSpec((1, TQ, D_GROUP), lambda b, i: (b, i, 0)),
                  pl.BlockSpec((1, S, D_GROUP), lambda b, i: (b, 0, 0)),
                  pl.BlockSpec((1, S, D_GROUP), lambda b, i: (b, 0, 0)),
                  _const_spec(table.shape)] + [_const_spec(p.shape) for p in vectors],
        out_specs=pl.BlockSpec((1, TQ, D_GROUP), lambda b, i: (b, i, 0)),
        out_shape=jax.ShapeDtypeStruct((B, S, D_GROUP), BF16),
        scratch_shapes=[pltpu.VMEM((2, TK, N_HEADS * TQ), F32),
                        pltpu.VMEM((n_kv, N_HEADS, V_AUG, TK), BF16),
                        pltpu.VMEM((N_MAPS, LANES, TQ), BF16),
                        pltpu.VMEM((2, TK, N_MAPS * TQ), F32),
                        pltpu.VMEM((2, 1, N_MAPS * TQ), F32),
                        pltpu.VMEM((2, 1, N_MAPS * TQ), F32),
                        pltpu.VMEM((N_HEADS, V_AUG, 2 * TQ), F32)],
        compiler_params=pltpu.CompilerParams(dimension_semantics=("arbitrary", "arbitrary"),
                                             vmem_limit_bytes=VMEM_LIMIT_BYTES),
        name="diff_attn",
    )(q, k, v, table, *vectors)


def _stream_cast(chunks, stage_ref, sem):
    def copy(c):
        return pltpu.make_async_copy(chunks[c][0], stage_ref.at[c % 2], sem.at[c % 2])

    copy(0).start()
    for c, (_, dst) in enumerate(chunks):
        if c + 1 < len(chunks):
            copy(c + 1).start()
        copy(c).wait()
        dst[...] = stage_ref[c % 2].astype(BF16)


def _mixer_out_kernel(layer, h_ref, ymix_ref, ydiff_ref, w_out_hbm, w1_hbm, w2_hbm, ln1_g_ref, ln1_b_ref,
                      ln2_g_ref, ln2_b_ref, o_ref, w_out_ref, w1_ref, w2_ref, stage_ref, sem):
    @pl.when(pl.program_id(0) == 0)
    def _():
        side = stage_ref.shape[1]
        cols = lambda c: pl.ds(c * side, side)
        chunks = [(w_out_hbm.at[layer], w_out_ref)]
        chunks += [(w1_hbm.at[layer, :, cols(c)], w1_ref.at[:, cols(c)]) for c in range(D_FF // side)]
        chunks += [(w2_hbm.at[layer, cols(c), :], w2_ref.at[cols(c), :]) for c in range(D_FF // side)]
        _stream_cast(chunks, stage_ref, sem)

    ln1_g, ln1_b, ln2_g, ln2_b = (r[layer:layer + 1, :] for r in (ln1_g_ref, ln1_b_ref, ln2_g_ref, ln2_b_ref))
    tm = h_ref.shape[0]
    halves = [slice(r, r + OUT_ROWS) for r in range(0, tm, OUT_ROWS)]
    mix = []
    for rows in halves:
        ymix = ymix_ref[rows, :]
        mix.append(jnp.dot(ymix[:, 0:D_GROUP], w_out_ref[0:D_GROUP, :], preferred_element_type=F32)
                   + jnp.dot(ydiff_ref[rows, :], w_out_ref[D_GROUP:2 * D_GROUP, :],
                             preferred_element_type=F32)
                   + jnp.dot(ymix[:, D_GROUP:3 * D_GROUP], w_out_ref[2 * D_GROUP:4 * D_GROUP, :],
                             preferred_element_type=F32))
    h1 = [_layer_norm(ALPHA * h_ref[rows, :] + m, ln1_g, ln1_b) for rows, m in zip(halves, mix)]
    act = []
    for x in h1:
        a = jnp.maximum(jnp.dot(x.astype(BF16), w1_ref[...], preferred_element_type=F32), 0.0)
        act.append((a * a).astype(BF16))
    ff = [jnp.dot(a, w2_ref[...], preferred_element_type=F32) for a in act]
    for rows, x, f in zip(halves, h1, ff):
        o_ref[rows, :] = _layer_norm(ALPHA * x + f, ln2_g, ln2_b)


def _mixer_out(layer, h, ymix, ydiff, weights, vectors):
    T = h.shape[0]
    tm = TM_OUT
    tok = lambda width: pl.BlockSpec((tm, width), lambda t: (t, 0))
    assert D_MODEL == WEIGHT_CHUNK and D_FF % WEIGHT_CHUNK == 0
    return pl.pallas_call(
        functools.partial(_mixer_out_kernel, layer),
        grid=(T // tm,),
        in_specs=([tok(D_MODEL), tok(3 * D_GROUP), tok(D_GROUP)]
                  + [pl.BlockSpec(memory_space=pl.ANY) for _ in weights]
                  + [_const_spec(p.shape) for p in vectors]),
        out_specs=tok(D_MODEL),
        out_shape=jax.ShapeDtypeStruct((T, D_MODEL), F32),
        scratch_shapes=[pltpu.VMEM((D_MODEL, D_MODEL), BF16),
                        pltpu.VMEM((D_MODEL, D_FF), BF16),
                        pltpu.VMEM((D_FF, D_MODEL), BF16),
                        pltpu.VMEM((2, WEIGHT_CHUNK, WEIGHT_CHUNK), F32),
                        pltpu.SemaphoreType.DMA((2,))],
        compiler_params=pltpu.CompilerParams(dimension_semantics=("arbitrary",),
                                             vmem_limit_bytes=VMEM_LIMIT_BYTES),
        name="mixer_out",
    )(h, ymix, ydiff, *weights, *vectors)


def _t5_bucket(n):
    max_exact = N_BUCKETS // 2
    large = max_exact + (jnp.log(jnp.maximum(n, 1).astype(F32) / max_exact)
                         / math.log(MAX_DISTANCE / max_exact) * (N_BUCKETS - max_exact)).astype(jnp.int32)
    large = jnp.minimum(large, N_BUCKETS - 1)
    return jnp.where(n < max_exact, n, large)


def _bias_table(rel_bias):
    assert TK >= MAX_DISTANCE, "tiles two or more behind the diagonal must see one constant bias"
    dist = jnp.arange(TQ + TK)
    onehot = (_t5_bucket(dist)[:, None] == jnp.arange(N_BUCKETS)[None, :]).astype(F32)
    rb = rel_bias.astype(F32)
    vals = jnp.sum(onehot[:, :, None] * (rb - rb[N_BUCKETS - 1])[None, :, :], axis=1) * LOG2E
    return jnp.concatenate([vals, jnp.full((TQ + TK, N_HEADS), MASK_VALUE, F32)], axis=0).T


def _block_diag(w):
    n_layers, g, c, d = w.shape
    out = jnp.zeros((n_layers, g * c, g * d), w.dtype)
    for i in range(g):
        out = out.at[:, i * c:(i + 1) * c, i * d:(i + 1) * d].set(w[:, i])
    return out


def kernel(x, emb_ln_g, emb_ln_b, rel_bias, w_in, conv_w, conv_b, conv_ln_g, conv_ln_b, conv_pw_w,
           conv_pw_b, lam_q1, lam_k1, lam_q2, lam_k2, diff_norm_g, sg_ln_g, sg_ln_b, sg_w, sg_b, pool_w,
           pool_scale, w_out, ln1_g, ln1_b, w_mlp1, w_mlp2, ln2_g, ln2_b):
    B, S, _ = x.shape
    assert S % TM_IN == 0 and (B * S) % TM_OUT == 0 and S % TQ == 0 and TQ == TK
    small = [conv_b, conv_ln_g, conv_ln_b, conv_pw_b, sg_ln_g, sg_ln_b, pool_scale, lam_q1, lam_k1,
             lam_q2, lam_k2, diff_norm_g, ln1_g, ln1_b, ln2_g, ln2_b]
    assert all(p.dtype == F32 and p.shape[0] == DEPTH and p.ndim == 2 for p in small)
    tril = jnp.tril(jnp.ones((CHUNK, CHUNK), F32))
    sg_w_cat = jnp.transpose(sg_w * tril, (0, 2, 1, 3)).reshape(DEPTH, CHUNK, N_SG_HEADS * CHUNK)
    sg_bias = jnp.repeat(jnp.swapaxes(sg_b, 1, 2), SG_HEAD_DIM, axis=2).astype(F32)
    in_matrices = [conv_w.astype(F32), conv_pw_w.astype(BF16), sg_w_cat.astype(BF16),
                   sg_bias, _block_diag(pool_w).astype(BF16)]
    in_vectors = [conv_b, conv_ln_g, conv_ln_b, conv_pw_b, sg_ln_g, sg_ln_b, pool_scale]
    attn_vectors = [lam_q1, lam_k1, lam_q2, lam_k2, jnp.tile(diff_norm_g, (1, N_HEADS))]
    out_matrices = [w_out, w_mlp1, w_mlp2]
    out_vectors = [ln1_g, ln1_b, ln2_g, ln2_b]
    table = _bias_table(rel_bias)
    emb_g, emb_b = emb_ln_g.reshape(1, -1).astype(F32), emb_ln_b.reshape(1, -1).astype(F32)

    h = x
    for l in range(DEPTH):
        outs = _mixer_in(l, h, emb_g, emb_b, w_in, in_matrices, in_vectors)
        if l == 0:
            h, q, k, v, ymix = outs
        else:
            q, k, v, ymix = outs
        ydiff = _diff_attn(l, q, k, v, table, attn_vectors)
        h = _mixer_out(l, h.reshape(B * S, D_MODEL), ymix.reshape(B * S, 3 * D_GROUP),
                       ydiff.reshape(B * S, D_GROUP), out_matrices, out_vectors).reshape(B, S, D_MODEL)
    return h
```

```python
import functools
import math
import types

import jax
import jax.numpy as jnp
import numpy as np
from jax import lax
from jax.experimental import pallas as pl
from jax.experimental.pallas import tpu as pltpu

F32 = jnp.float32
BF16 = jnp.bfloat16

D_MODEL = 1024
DEPTH = 2
D_GROUP = 256
CONV_WIDTH = 31
N_HEADS = 4
QK_DIM = 32
V_DIM = 64
N_SG_HEADS = 4
SG_HEAD_DIM = 64
CHUNK = 128
POOL_WINDOWS = (2, 4, 8, 16)
POOL_GROUP_DIM = 64
D_FF = 4096
N_BUCKETS = 32
MAX_DISTANCE = 128
LN_EPS = 1e-5
ALPHA = (2.0 * DEPTH) ** 0.25
D_IN = 2048
O_A, O_Q, O_K, O_V, O_U, O_SV, O_P = 0, 512, 768, 1024, 1280, 1536, 1792

LANES = 128
SUBLANES = 8
VMEM_LIMIT_BYTES = 56 * 1024 * 1024

TM_IN = 1024
TM_OUT = 512
OUT_ROWS = 256
WEIGHT_CHUNK = 1024
TQ = 256
TK = 256
HALO = 32
ROW_CHUNK = 64
V_AUG = 80
N_MAPS = 2 * N_HEADS
MASK_VALUE = -1e30
LOG2E = math.log2(math.e)
Q_SCALE = (QK_DIM ** -0.5) * LOG2E


def _layer_norm(x, g, b):
    mu = jnp.mean(x, axis=-1, keepdims=True)
    xc = x - mu
    var = jnp.mean(xc * xc, axis=-1, keepdims=True)
    return xc * lax.rsqrt(var + LN_EPS) * g + b


def _gelu_tanh(x):
    c = math.sqrt(2.0 / math.pi)
    return 0.5 * x * (1.0 + jnp.tanh(c * (x + 0.044715 * (x * x * x))))


def _sigmoid(x):
    return 1.0 / (1.0 + jnp.exp(-x))


def _runtime_zero_bits(v):
    bits = pltpu.bitcast(v, jnp.int32)
    return lax.shift_right_logical(lax.shift_right_logical(bits, 16), 16)


def _runtime_zero(v):
    return _runtime_zero_bits(v).astype(F32)


def _mixer_in_kernel(layer, x_ref, emb_g_ref, emb_b_ref, w_in_hbm, conv_w_ref, pw_w_ref, sg_w_ref,
                     sg_bias_ref, pool_w_ref, *rest):
    first = layer == 0
    conv_b, cln_g, cln_b, pw_b, sgln_g, sgln_b, pool_scale = (r[layer:layer + 1, :] for r in rest[:7])
    rest = rest[7:]
    if first:
        h_ref, *rest = rest
    (q_ref, k_ref, v_ref, ymix_ref, w_in_ref, stage_ref, sem, glu_buf, shift_buf, pool_buf, win_buf,
     conv_buf, gu_buf, gv_buf) = rest
    tm = x_ref.shape[1]
    j = pl.program_id(1)

    @pl.when((pl.program_id(0) == 0) & (j == 0))
    def _():
        side = stage_ref.shape[1]
        cols = lambda c: pl.ds(c * side, side)
        _stream_cast([(w_in_hbm.at[layer, :, cols(c)], w_in_ref.at[:, cols(c)])
                      for c in range(D_IN // side)], stage_ref, sem)

    @pl.when(j == 0)
    def _():
        glu_buf[0:HALO, :] = jnp.zeros((HALO, D_GROUP), F32)
        pool_buf[0:HALO, :] = jnp.zeros((HALO, D_GROUP), F32)

    @pl.when(j > 0)
    def _():
        glu_buf[0:HALO, :] = glu_buf[tm:tm + HALO, :]
        pool_buf[0:HALO, :] = pool_buf[tm:tm + HALO, :]

    x = x_ref[0]
    if first:
        h = _layer_norm(x, emb_g_ref[...], emb_b_ref[...])
        h_ref[0] = h
    else:
        h = x
    hb = h.astype(BF16)

    def first_vreg(value):
        return value[0:SUBLANES, 0:LANES]

    def proj(lo, hi, after=None):
        w = w_in_ref[:, lo:hi]
        if after is not None:
            bits = pltpu.bitcast(w, jnp.int32)
            zero = _runtime_zero_bits(after)
            bits = bits | jnp.tile(zero, (bits.shape[0] // SUBLANES, bits.shape[1] // LANES))
            w = pltpu.bitcast(bits, BF16)
        return jnp.dot(hb, w, preferred_element_type=F32)

    n = HALO + tm
    n_shift = n - SUBLANES
    quarter = tm // 4

    def conv_rows(r_lo, r_hi, after):
        for r0 in range(r_lo, r_hi, ROW_CHUNK):
            acc = jnp.broadcast_to(conv_b, (ROW_CHUNK, D_GROUP))
            if r0 == r_lo:
                acc = acc + jnp.tile(_runtime_zero(after), (ROW_CHUNK // SUBLANES, D_GROUP // LANES))
            for tap in range(CONV_WIDTH):
                off = HALO + r0 - (CONV_WIDTH - 1) + tap
                base, s = off - off % SUBLANES, off % SUBLANES
                if s == 0:
                    rows = glu_buf[base:base + ROW_CHUNK, :]
                else:
                    assert base + ROW_CHUNK <= n_shift
                    rows = shift_buf[s - 1, base:base + ROW_CHUNK, :]
                acc = acc + conv_w_ref[tap:tap + 1, :] * rows
            conv_buf[r0:r0 + ROW_CHUNK, :] = acc
        return first_vreg(acc)

    za = proj(O_A, O_Q)
    glu_buf[HALO:n, :] = za[:, :D_GROUP] * _sigmoid(za[:, D_GROUP:])
    q_ref[0] = (proj(O_Q, O_K) * Q_SCALE).astype(BF16)
    for s in range(1, SUBLANES):
        shift_buf[s - 1] = glu_buf[s:s + n_shift, :]

    zk = proj(O_K, O_V)
    k_ref[0] = zk.astype(BF16)
    c0 = conv_rows(0, quarter, first_vreg(zk))
    zv = proj(O_V, O_U)
    v_ref[0] = zv.astype(BF16)
    c1 = conv_rows(quarter, 2 * quarter, first_vreg(zv))
    zu = proj(O_U, O_SV, after=c0)
    gu_buf[...] = zu
    c2 = conv_rows(2 * quarter, 3 * quarter, first_vreg(zu))
    zsv = proj(O_SV, O_P, after=c1)
    gv_buf[...] = zsv
    conv_rows(3 * quarter, tm, first_vreg(zsv))
    pool_buf[HALO:n, :] = proj(O_P, D_IN, after=c2)

    hc = _layer_norm(conv_buf[...], cln_g, cln_b)
    hc = hc * _sigmoid(hc)
    y_conv = jnp.dot(hc.astype(BF16), pw_w_ref[...], preferred_element_type=F32) + pw_b
    ymix_ref[0, :, 0:D_GROUP] = y_conv.astype(BF16)

    gu = _gelu_tanh(gu_buf[...])
    gv = _layer_norm(_gelu_tanh(gv_buf[...]), sgln_g, sgln_b).astype(BF16)
    lane = lax.broadcasted_iota(jnp.int32, (CHUNK, D_GROUP), 1)
    sg_w = sg_w_ref[...]
    sg_bias = sg_bias_ref[...]
    for c in range(tm // CHUNK):
        vc = gv[c * CHUNK:(c + 1) * CHUNK, :]
        stacked = jnp.concatenate(
            [jnp.where((lane >= hd * SG_HEAD_DIM) & (lane < (hd + 1) * SG_HEAD_DIM), vc,
                       jnp.zeros_like(vc)) for hd in range(N_SG_HEADS)], axis=0)
        mixed = jnp.dot(sg_w, stacked, preferred_element_type=F32) + sg_bias
        ymix_ref[0, c * CHUNK:(c + 1) * CHUNK, D_GROUP:2 * D_GROUP] = (
            gu[c * CHUNK:(c + 1) * CHUNK, :] * mixed).astype(BF16)

    assert POOL_WINDOWS == (2, 4, 8, 16) and HALO == 32
    win_buf[0, 8:n, :] = pool_buf[8:n, :] + pool_buf[7:n - 1, :]
    win_buf[1, 16:n, :] = win_buf[0, 16:n, :] + win_buf[0, 14:n - 2, :]
    win_buf[2, 24:n, :] = win_buf[1, 24:n, :] + win_buf[1, 20:n - 4, :]
    sum16 = win_buf[2, HALO:n, :] + win_buf[2, HALO - 8:n - 8, :]
    lane_r = lax.broadcasted_iota(jnp.int32, (tm, D_GROUP), 1)
    row_r = lax.broadcasted_iota(jnp.int32, (tm, D_GROUP), 0)
    group = lane_r // POOL_GROUP_DIM
    window = jnp.where(group == 0, POOL_WINDOWS[0],
                       jnp.where(group == 1, POOL_WINDOWS[1],
                                 jnp.where(group == 2, POOL_WINDOWS[2], POOL_WINDOWS[3])))
    total = jnp.where(group == 0, win_buf[0, HALO:n, :],
                      jnp.where(group == 1, win_buf[1, HALO:n, :],
                                jnp.where(group == 2, win_buf[2, HALO:n, :], sum16)))
    cnt = jnp.minimum(j * tm + row_r + 1, window).astype(F32)
    pooled = total / cnt - pool_buf[HALO:n, :]
    y_pool = jnp.dot(pooled.astype(BF16), pool_w_ref[...], preferred_element_type=F32)
    ymix_ref[0, :, 2 * D_GROUP:3 * D_GROUP] = (y_pool * pool_scale).astype(BF16)


def _const_spec(shape):
    return pl.BlockSpec(shape, lambda *_: (0,) * len(shape))


def _layer_spec(stacked, layer, **kwargs):
    zeros = (0,) * (stacked.ndim - 1)
    return pl.BlockSpec((None,) + stacked.shape[1:], lambda *_: (layer,) + zeros, **kwargs)


def _mixer_in(layer, x, emb_g, emb_b, w_in, stacked, vectors):
    first = layer == 0
    assert D_MODEL == WEIGHT_CHUNK and D_IN % WEIGHT_CHUNK == 0
    B, S, _ = x.shape
    tm = TM_IN
    tok = lambda width: pl.BlockSpec((1, tm, width), lambda b, j: (b, j, 0))
    out_shape = [jax.ShapeDtypeStruct((B, S, D_GROUP), BF16),
                 jax.ShapeDtypeStruct((B, S, D_GROUP), BF16),
                 jax.ShapeDtypeStruct((B, S, D_GROUP), BF16),
                 jax.ShapeDtypeStruct((B, S, 3 * D_GROUP), BF16)]
    out_specs = [tok(D_GROUP), tok(D_GROUP), tok(D_GROUP), tok(3 * D_GROUP)]
    if first:
        out_shape = [jax.ShapeDtypeStruct((B, S, D_MODEL), F32)] + out_shape
        out_specs = [tok(D_MODEL)] + out_specs
    return pl.pallas_call(
        functools.partial(_mixer_in_kernel, layer),
        grid=(B, S // tm),
        in_specs=([tok(D_MODEL), _const_spec(emb_g.shape), _const_spec(emb_b.shape),
                   pl.BlockSpec(memory_space=pl.ANY)]
                  + [_layer_spec(p, layer) for p in stacked] + [_const_spec(p.shape) for p in vectors]),
        out_specs=out_specs,
        out_shape=out_shape,
        scratch_shapes=[pltpu.VMEM((D_MODEL, D_IN), BF16),
                        pltpu.VMEM((2, WEIGHT_CHUNK, WEIGHT_CHUNK), F32),
                        pltpu.SemaphoreType.DMA((2,)),
                        pltpu.VMEM((HALO + tm, D_GROUP), F32),
                        pltpu.VMEM((SUBLANES - 1, HALO + tm - SUBLANES, D_GROUP), F32),
                        pltpu.VMEM((HALO + tm, D_GROUP), F32),
                        pltpu.VMEM((3, HALO + tm, D_GROUP), F32),
                        pltpu.VMEM((tm, D_GROUP), F32),
                        pltpu.VMEM((tm, D_GROUP), F32),
                        pltpu.VMEM((tm, D_GROUP), F32)],
        compiler_params=pltpu.CompilerParams(dimension_semantics=("arbitrary", "arbitrary"),
                                             vmem_limit_bytes=VMEM_LIMIT_BYTES),
        name="mixer_in",
    )(x, emb_g, emb_b, w_in, *stacked, *vectors)


def _diff_attn_kernel(layer, q_ref, k_ref, v_ref, table_ref, lq1_ref, lk1_ref, lq2_ref, lk2_ref, g_ref,
                      o_ref, bias_ref, vt_ref, qm_ref, s_ref, m_ref, a_ref, acc_ref):
    lam_init = 0.8 - 0.6 * math.exp(-0.3 * layer)
    c = pl.program_id(1)
    n_kv = k_ref.shape[1] // TK

    @pl.when((pl.program_id(0) == 0) & (c == 0))
    def _():
        for hd in range(N_HEADS):
            spread = jnp.broadcast_to(table_ref[hd:hd + 1, :], (TK, table_ref.shape[1]))
            sheared = pltpu.roll(spread, 0, axis=1, stride=1, stride_axis=0)
            bias_ref[0, :, hd * TQ:(hd + 1) * TQ] = sheared[:, 0:TQ]
            bias_ref[1, :, hd * TQ:(hd + 1) * TQ] = sheared[:, TK:TK + TQ]

    @pl.when(c == 0)
    def _():
        row = lax.broadcasted_iota(jnp.int32, (V_AUG - V_DIM, TK), 0)
        ones_rows = jnp.where(row == 0, 1.0, 0.0).astype(BF16)
        for jt in range(n_kv):
            vt = v_ref[0, jt * TK:(jt + 1) * TK, :].astype(F32).T
            for hd in range(N_HEADS):
                vt_ref[jt, hd, 0:V_DIM, :] = vt[hd * V_DIM:(hd + 1) * V_DIM, :].astype(BF16)
                vt_ref[jt, hd, V_DIM:V_AUG, :] = ones_rows

    st = types.SimpleNamespace()

    def use(parity):
        st.qm, st.s, st.m, st.a, st.acc = (r.at[parity] for r in (qm_ref, s_ref, m_ref, a_ref, acc_ref))

    def start_query_tile(q_rows):
        q = q_ref[0, q_rows, :].astype(F32)
        row = lax.broadcasted_iota(jnp.int32, (LANES, TQ), 0)
        for mp in range(2):
            qt = q[:, mp * LANES:(mp + 1) * LANES].T
            for hd in range(N_HEADS):
                keep = (row >= hd * QK_DIM) & (row < (hd + 1) * QK_DIM)
                st.qm[hd * 2 + mp] = jnp.where(keep, qt, 0.0).astype(BF16)
        st.m[1] = jnp.full(st.m.shape[1:], MASK_VALUE, F32)
        st.acc[...] = jnp.zeros(st.acc.shape, F32)

    def scores_into(slot, t, bias, hd):
        rows = pl.ds(pl.multiple_of(t * TK, TK), TK)
        for mp in range(2):
            blk = hd * 2 + mp
            cols = slice(blk * TQ, (blk + 1) * TQ)
            sc = jnp.dot(k_ref[0, rows, mp * LANES:(mp + 1) * LANES], st.qm[blk],
                         preferred_element_type=F32)
            if bias is not None:
                sc = sc + bias_ref[bias, :, hd * TQ:(hd + 1) * TQ]
            st.s[slot, :, cols] = sc
            m_old = st.m[1 - slot, :, cols]
            m_new = jnp.maximum(m_old, jnp.max(sc, axis=0, keepdims=True))
            st.m[slot, :, cols] = m_new
            st.a[slot, :, cols] = jnp.exp2(m_old - m_new)

    def values_from(slot, t, hd):
        cols = slice(hd * 2 * TQ, (hd + 1) * 2 * TQ)
        p = jnp.exp2(st.s[slot, :, cols] - st.m[slot, :, cols]).astype(BF16)
        pv = jnp.dot(vt_ref[t, hd], p, preferred_element_type=F32)
        st.acc[hd] = st.acc[hd] * st.a[slot, :, cols] + pv

    def stage(score_args, value_args):
        for hd in range(N_HEADS):
            if score_args is not None:
                scores_into(*score_args, hd)
            if value_args is not None:
                values_from(*value_args, hd)

    def last_values_and_finalize(slot, q_rows):
        row = slice(layer, layer + 1)
        lam = (jnp.exp(jnp.sum(lq1_ref[row, :] * lk1_ref[row, :], keepdims=True))
               - jnp.exp(jnp.sum(lq2_ref[row, :] * lk2_ref[row, :], keepdims=True)) + lam_init)
        outs = []
        for hd in range(N_HEADS):
            values_from(slot, 0, hd)
            a = st.acc[hd]
            o1 = a[0:V_DIM, 0:TQ] / a[V_DIM:V_DIM + 1, 0:TQ]
            o2 = a[0:V_DIM, TQ:2 * TQ] / a[V_DIM:V_DIM + 1, TQ:2 * TQ]
            o = o1 - lam * o2
            ms = jnp.mean(o * o, axis=0, keepdims=True)
            outs.append(o * lax.rsqrt(ms + LN_EPS) * (1.0 - lam_init))
        o_ref[0, q_rows, :] = (jnp.concatenate(outs, axis=0).T * g_ref[row, :]).astype(BF16)

    def first_two_steps(i):
        stage((0, i, 0), None)
        stage((1, i - 1, 1), (0, i))

    def far_pairs(i, n_pairs):
        def far_pair(k2, carry):
            t = i - 2 - 2 * k2
            stage((0, t, None), (1, t + 1))
            stage((1, t - 1, None), (0, t))
            return carry
        lax.fori_loop(0, n_pairs, far_pair, 0)

    even_rows, odd_rows = slice(0, TQ), slice(TQ, 2 * TQ)

    def finish_even_start_odd():
        use(1)
        start_query_tile(odd_rows)
        first_two_steps(2 * c + 1)
        use(0)
        last_values_and_finalize(0, even_rows)

    use(0)
    start_query_tile(even_rows)

    @pl.when(c == 0)
    def _():
        use(0)
        stage((0, 0, 0), None)
        finish_even_start_odd()

    @pl.when(c >= 1)
    def _():
        use(0)
        first_two_steps(2 * c)

    use(0)
    far_pairs(2 * c, jnp.maximum(c - 1, 0))

    @pl.when(c >= 1)
    def _():
        use(0)
        stage((0, 0, None), (1, 1))
        finish_even_start_odd()

    use(1)
    far_pairs(2 * c + 1, c)
    last_values_and_finalize(1, odd_rows)


def _diff_attn(layer, q, k, v, table, vectors):
    B, S, _ = q.shape
    n_kv = S // TK
    return pl.pallas_call(
        functools.partial(_diff_attn_kernel, layer),
        grid=(B, S // (2 * TQ)),
        in_specs=[pl.BlockSpec((1, 2 * TQ, D_GROUP), lambda b, c: (b, c, 0)),
                  pl.BlockSpec((1, S, D_GROUP), lambda b, c: (b, 0, 0)),
                  pl.BlockSpec((1, S, D_GROUP), lambda b, c: (b, 0, 0)),
                  _const_spec(table.shape)] + [_const_spec(p.shape) for p in vectors],
        out_specs=pl.BlockSpec((1, 2 * TQ, D_GROUP), lambda b, c: (b, c, 0)),
        out_shape=jax.ShapeDtypeStruct((B, S, D_GROUP), BF16),
        scratch_shapes=[pltpu.VMEM((2, TK, N_HEADS * TQ), F32),
                        pltpu.VMEM((n_kv, N_HEADS, V_AUG, TK), BF16),
                        pltpu.VMEM((2, N_MAPS, LANES, TQ), BF16),
                        pltpu.VMEM((2, 2, TK, N_MAPS * TQ), F32),
                        pltpu.VMEM((2, 2, 1, N_MAPS * TQ), F32),
                        pltpu.VMEM((2, 2, 1, N_MAPS * TQ), F32),
                        pltpu.VMEM((2, N_HEADS, V_AUG, 2 * TQ), F32)],
        compiler_params=pltpu.CompilerParams(dimension_semantics=("arbitrary", "arbitrary"),
                                             vmem_limit_bytes=VMEM_LIMIT_BYTES),
        name="diff_attn",
    )(q, k, v, table, *vectors)


def _stream_cast(chunks, stage_ref, sem):
    def copy(c):
        return pltpu.make_async_copy(chunks[c][0], stage_ref.at[c % 2], sem.at[c % 2])

    copy(0).start()
    for c, (_, dst) in enumerate(chunks):
        if c + 1 < len(chunks):
            copy(c + 1).start()
        copy(c).wait()
        dst[...] = stage_ref[c % 2].astype(BF16)


def _mixer_out_kernel(layer, h_ref, ymix_ref, ydiff_ref, w_out_hbm, w1_hbm, w2_hbm, ln1_g_ref, ln1_b_ref,
                      ln2_g_ref, ln2_b_ref, o_ref, w_out_ref, w1_ref, w2_ref, stage_ref, sem):
    @pl.when(pl.program_id(0) == 0)
    def _():
        side = stage_ref.shape[1]
        cols = lambda c: pl.ds(c * side, side)
        chunks = [(w_out_hbm.at[layer], w_out_ref)]
        chunks += [(w1_hbm.at[layer, :, cols(c)], w1_ref.at[:, cols(c)]) for c in range(D_FF // side)]
        chunks += [(w2_hbm.at[layer, cols(c), :], w2_ref.at[cols(c), :]) for c in range(D_FF // side)]
        _stream_cast(chunks, stage_ref, sem)

    ln1_g, ln1_b, ln2_g, ln2_b = (r[layer:layer + 1, :] for r in (ln1_g_ref, ln1_b_ref, ln2_g_ref, ln2_b_ref))
    tm = h_ref.shape[0]
    halves = [slice(r, r + OUT_ROWS) for r in range(0, tm, OUT_ROWS)]
    mix = []
    for rows in halves:
        ymix = ymix_ref[rows, :]
        mix.append(jnp.dot(ymix[:, 0:D_GROUP], w_out_ref[0:D_GROUP, :], preferred_element_type=F32)
                   + jnp.dot(ydiff_ref[rows, :], w_out_ref[D_GROUP:2 * D_GROUP, :],
                             preferred_element_type=F32)
                   + jnp.dot(ymix[:, D_GROUP:3 * D_GROUP], w_out_ref[2 * D_GROUP:4 * D_GROUP, :],
                             preferred_element_type=F32))
    h1 = [_layer_norm(ALPHA * h_ref[rows, :] + m, ln1_g, ln1_b) for rows, m in zip(halves, mix)]
    act = []
    for x in h1:
        a = jnp.maximum(jnp.dot(x.astype(BF16), w1_ref[...], preferred_element_type=F32), 0.0)
        act.append((a * a).astype(BF16))
    ff = [jnp.dot(a, w2_ref[...], preferred_element_type=F32) for a in act]
    for rows, x, f in zip(halves, h1, ff):
        o_ref[rows, :] = _layer_norm(ALPHA * x + f, ln2_g, ln2_b)


def _mixer_out(layer, h, ymix, ydiff, weights, vectors):
    T = h.shape[0]
    tm = TM_OUT
    tok = lambda width: pl.BlockSpec((tm, width), lambda t: (t, 0))
    assert D_MODEL == WEIGHT_CHUNK and D_FF % WEIGHT_CHUNK == 0
    return pl.pallas_call(
        functools.partial(_mixer_out_kernel, layer),
        grid=(T // tm,),
        in_specs=([tok(D_MODEL), tok(3 * D_GROUP), tok(D_GROUP)]
                  + [pl.BlockSpec(memory_space=pl.ANY) for _ in weights]
                  + [_const_spec(p.shape) for p in vectors]),
        out_specs=tok(D_MODEL),
        out_shape=jax.ShapeDtypeStruct((T, D_MODEL), F32),
        scratch_shapes=[pltpu.VMEM((D_MODEL, D_MODEL), BF16),
                        pltpu.VMEM((D_MODEL, D_FF), BF16),
                        pltpu.VMEM((D_FF, D_MODEL), BF16),
                        pltpu.VMEM((2, WEIGHT_CHUNK, WEIGHT_CHUNK), F32),
                        pltpu.SemaphoreType.DMA((2,))],
        compiler_params=pltpu.CompilerParams(dimension_semantics=("arbitrary",),
                                             vmem_limit_bytes=VMEM_LIMIT_BYTES),
        name="mixer_out",
    )(h, ymix, ydiff, *weights, *vectors)


def _t5_bucket(n):
    max_exact = N_BUCKETS // 2
    large = max_exact + (jnp.log(jnp.maximum(n, 1).astype(F32) / max_exact)
                         / math.log(MAX_DISTANCE / max_exact) * (N_BUCKETS - max_exact)).astype(jnp.int32)
    large = jnp.minimum(large, N_BUCKETS - 1)
    return jnp.where(n < max_exact, n, large)


def _bias_table(rel_bias):
    assert TK >= MAX_DISTANCE, "tiles two or more behind the diagonal must see one constant bias"
    dist = jnp.arange(TQ + TK)
    onehot = (_t5_bucket(dist)[:, None] == jnp.arange(N_BUCKETS)[None, :]).astype(F32)
    rb = rel_bias.astype(F32)
    vals = jnp.sum(onehot[:, :, None] * (rb - rb[N_BUCKETS - 1])[None, :, :], axis=1) * LOG2E
    return jnp.concatenate([vals, jnp.full((TQ + TK, N_HEADS), MASK_VALUE, F32)], axis=0).T


def _block_diag(w):
    n_layers, g, c, d = w.shape
    out = jnp.zeros((n_layers, g * c, g * d), w.dtype)
    for i in range(g):
        out = out.at[:, i * c:(i + 1) * c, i * d:(i + 1) * d].set(w[:, i])
    return out


def kernel(x, emb_ln_g, emb_ln_b, rel_bias, w_in, conv_w, conv_b, conv_ln_g, conv_ln_b, conv_pw_w,
           conv_pw_b, lam_q1, lam_k1, lam_q2, lam_k2, diff_norm_g, sg_ln_g, sg_ln_b, sg_w, sg_b, pool_w,
           pool_scale, w_out, ln1_g, ln1_b, w_mlp1, w_mlp2, ln2_g, ln2_b):
    B, S, _ = x.shape
    assert S % TM_IN == 0 and (B * S) % TM_OUT == 0 and S % TQ == 0 and TQ == TK
    small = [conv_b, conv_ln_g, conv_ln_b, conv_pw_b, sg_ln_g, sg_ln_b, pool_scale, lam_q1, lam_k1,
             lam_q2, lam_k2, diff_norm_g, ln1_g, ln1_b, ln2_g, ln2_b]
    assert all(p.dtype == F32 and p.shape[0] == DEPTH and p.ndim == 2 for p in small)
    tril = jnp.tril(jnp.ones((CHUNK, CHUNK), F32))
    sg_w_cat = jnp.transpose(sg_w * tril, (0, 2, 1, 3)).reshape(DEPTH, CHUNK, N_SG_HEADS * CHUNK)
    sg_bias = jnp.repeat(jnp.swapaxes(sg_b, 1, 2), SG_HEAD_DIM, axis=2).astype(F32)
    in_matrices = [conv_w.astype(F32), conv_pw_w.astype(BF16), sg_w_cat.astype(BF16),
                   sg_bias, _block_diag(pool_w).astype(BF16)]
    in_vectors = [conv_b, conv_ln_g, conv_ln_b, conv_pw_b, sg_ln_g, sg_ln_b, pool_scale]
    attn_vectors = [lam_q1, lam_k1, lam_q2, lam_k2, jnp.tile(diff_norm_g, (1, N_HEADS))]
    out_matrices = [w_out, w_mlp1, w_mlp2]
    out_vectors = [ln1_g, ln1_b, ln2_g, ln2_b]
    table = _bias_table(rel_bias)
    emb_g, emb_b = emb_ln_g.reshape(1, -1).astype(F32), emb_ln_b.reshape(1, -1).astype(F32)

    h = x
    for l in range(DEPTH):
        outs = _mixer_in(l, h, emb_g, emb_b, w_in, in_matrices, in_vectors)
        if l == 0:
            h, q, k, v, ymix = outs
        else:
            q, k, v, ymix = outs
        ydiff = _diff_attn(l, q, k, v, table, attn_vectors)
        h = _mixer_out(l, h.reshape(B * S, D_MODEL), ymix.reshape(B * S, 3 * D_GROUP),
                       ydiff.reshape(B * S, D_GROUP), out_matrices, out_vectors).reshape(B, S, D_MODEL)
    return h
```

```python
import functools
import math
import types

import jax
import jax.numpy as jnp
from jax import lax
from jax.experimental import pallas as pl
from jax.experimental.pallas import tpu as pltpu

F32 = jnp.float32
BF16 = jnp.bfloat16

D_MODEL = 1024
DEPTH = 2
D_GROUP = 256
CONV_WIDTH = 31
N_HEADS = 4
QK_DIM = 32
V_DIM = 64
N_SG_HEADS = 4
SG_HEAD_DIM = 64
CHUNK = 128
POOL_WINDOWS = (2, 4, 8, 16)
POOL_GROUP_DIM = 64
D_FF = 4096
N_BUCKETS = 32
MAX_DISTANCE = 128
LN_EPS = 1e-5
ALPHA = (2.0 * DEPTH) ** 0.25
D_IN = 2048
O_A, O_Q, O_K, O_V, O_U, O_SV, O_P = 0, 512, 768, 1024, 1280, 1536, 1792

LANES = 128
SUBLANES = 8
VMEM_LIMIT_BYTES = 56 * 1024 * 1024

TM_IN = 1024
TM_OUT = 512
OUT_ROWS = 256
WEIGHT_CHUNK = 1024
TQ = 256
TK = 256
Q_PER_STEP = 4
HALO = 32
ROW_CHUNK = 64
V_AUG = 80
N_MAPS = 2 * N_HEADS
MASK_VALUE = -1e30
LOG2E = math.log2(math.e)
Q_SCALE = (QK_DIM ** -0.5) * LOG2E


def _layer_norm(x, g, b):
    mu = jnp.mean(x, axis=-1, keepdims=True)
    xc = x - mu
    var = jnp.mean(xc * xc, axis=-1, keepdims=True)
    return xc * lax.rsqrt(var + LN_EPS) * g + b


def _gelu_tanh(x):
    c = math.sqrt(2.0 / math.pi)
    return 0.5 * x * (1.0 + jnp.tanh(c * (x + 0.044715 * (x * x * x))))


def _sigmoid(x):
    return 1.0 / (1.0 + jnp.exp(-x))


def _runtime_zero_bits(v):
    bits = pltpu.bitcast(v, jnp.int32)
    return lax.shift_right_logical(lax.shift_right_logical(bits, 16), 16)


def _runtime_zero(v):
    return _runtime_zero_bits(v).astype(F32)


def _stream_cast(chunks, stage_ref, sem):
    def copy(c):
        return pltpu.make_async_copy(chunks[c][0], stage_ref.at[c % 2], sem.at[c % 2])

    copy(0).start()
    for c, (_, dst) in enumerate(chunks):
        if c + 1 < len(chunks):
            copy(c + 1).start()
        copy(c).wait()
        dst[...] = stage_ref[c % 2].astype(BF16)


def _mixer_in_kernel(layer, x_ref, emb_g_ref, emb_b_ref, w_in_hbm, conv_w_ref, pw_w_ref, pool_w_ref,
                     *rest):
    first = layer == 0
    conv_b, cln_g, cln_b, pw_b, pool_scale = (r[layer:layer + 1, :] for r in rest[:5])
    rest = rest[5:]
    if first:
        h_ref, *rest = rest
    (q_ref, k_ref, v_ref, ymix_ref, gate_ref, w_in_ref, stage_ref, sem, glu_buf, shift_buf, pool_buf,
     win_buf, conv_buf) = rest
    tm = x_ref.shape[1]
    j = pl.program_id(1)

    @pl.when((pl.program_id(0) == 0) & (j == 0))
    def _():
        side = stage_ref.shape[1]
        cols = lambda c: pl.ds(c * side, side)
        _stream_cast([(w_in_hbm.at[layer, :, cols(c)], w_in_ref.at[:, cols(c)])
                      for c in range(D_IN // side)], stage_ref, sem)

    @pl.when(j == 0)
    def _():
        glu_buf[0:HALO, :] = jnp.zeros((HALO, D_GROUP), F32)
        pool_buf[0:HALO, :] = jnp.zeros((HALO, D_GROUP), F32)

    @pl.when(j > 0)
    def _():
        glu_buf[0:HALO, :] = glu_buf[tm:tm + HALO, :]
        pool_buf[0:HALO, :] = pool_buf[tm:tm + HALO, :]

    x = x_ref[0]
    if first:
        h = _layer_norm(x, emb_g_ref[...], emb_b_ref[...])
        h_ref[0] = h
    else:
        h = x
    hb = h.astype(BF16)

    def first_vreg(value):
        return value[0:SUBLANES, 0:LANES]

    def proj(lo, hi, after=None):
        w = w_in_ref[:, lo:hi]
        if after is not None:
            bits = pltpu.bitcast(w, jnp.int32)
            zero = _runtime_zero_bits(after)
            bits = bits | jnp.tile(zero, (bits.shape[0] // SUBLANES, bits.shape[1] // LANES))
            w = pltpu.bitcast(bits, BF16)
        return jnp.dot(hb, w, preferred_element_type=F32)

    n = HALO + tm
    n_shift = n - SUBLANES
    quarter = tm // 4

    def conv_rows(r_lo, r_hi, after):
        for r0 in range(r_lo, r_hi, ROW_CHUNK):
            acc = jnp.broadcast_to(conv_b, (ROW_CHUNK, D_GROUP))
            if r0 == r_lo:
                acc = acc + jnp.tile(_runtime_zero(after), (ROW_CHUNK // SUBLANES, D_GROUP // LANES))
            for tap in range(CONV_WIDTH):
                off = HALO + r0 - (CONV_WIDTH - 1) + tap
                base, s = off - off % SUBLANES, off % SUBLANES
                if s == 0:
                    rows = glu_buf[base:base + ROW_CHUNK, :]
                else:
                    assert base + ROW_CHUNK <= n_shift
                    rows = shift_buf[s - 1, base:base + ROW_CHUNK, :]
                acc = acc + conv_w_ref[tap:tap + 1, :] * rows
            conv_buf[r0:r0 + ROW_CHUNK, :] = acc
        return first_vreg(acc)

    za = proj(O_A, O_Q)
    glu_buf[HALO:n, :] = za[:, :D_GROUP] * _sigmoid(za[:, D_GROUP:])
    q_ref[0] = (proj(O_Q, O_K) * Q_SCALE).astype(BF16)
    for s in range(1, SUBLANES):
        shift_buf[s - 1] = glu_buf[s:s + n_shift, :]

    zk = proj(O_K, O_V)
    k_ref[0] = zk.astype(BF16)
    c0 = conv_rows(0, quarter, first_vreg(zk))
    zv = proj(O_V, O_U)
    v_ref[0] = zv.astype(BF16)
    c1 = conv_rows(quarter, 2 * quarter, first_vreg(zv))
    zu = proj(O_U, O_SV, after=c0)
    gate_ref[0, :, 0:D_GROUP] = zu
    c2 = conv_rows(2 * quarter, 3 * quarter, first_vreg(zu))
    zsv = proj(O_SV, O_P, after=c1)
    gate_ref[0, :, D_GROUP:2 * D_GROUP] = zsv
    conv_rows(3 * quarter, tm, first_vreg(zsv))
    pool_buf[HALO:n, :] = proj(O_P, D_IN, after=c2)

    hc = _layer_norm(conv_buf[...], cln_g, cln_b)
    hc = hc * _sigmoid(hc)
    y_conv = jnp.dot(hc.astype(BF16), pw_w_ref[...], preferred_element_type=F32) + pw_b
    ymix_ref[0, :, 0:D_GROUP] = y_conv.astype(BF16)

    assert POOL_WINDOWS == (2, 4, 8, 16) and HALO == 32
    win_buf[0, 8:n, :] = pool_buf[8:n, :] + pool_buf[7:n - 1, :]
    win_buf[1, 16:n, :] = win_buf[0, 16:n, :] + win_buf[0, 14:n - 2, :]
    win_buf[2, 24:n, :] = win_buf[1, 24:n, :] + win_buf[1, 20:n - 4, :]
    sum16 = win_buf[2, HALO:n, :] + win_buf[2, HALO - 8:n - 8, :]
    lane_r = lax.broadcasted_iota(jnp.int32, (tm, D_GROUP), 1)
    row_r = lax.broadcasted_iota(jnp.int32, (tm, D_GROUP), 0)
    group = lane_r // POOL_GROUP_DIM
    window = jnp.where(group == 0, POOL_WINDOWS[0],
                       jnp.where(group == 1, POOL_WINDOWS[1],
                                 jnp.where(group == 2, POOL_WINDOWS[2], POOL_WINDOWS[3])))
    total = jnp.where(group == 0, win_buf[0, HALO:n, :],
                      jnp.where(group == 1, win_buf[1, HALO:n, :],
                                jnp.where(group == 2, win_buf[2, HALO:n, :], sum16)))
    cnt = jnp.minimum(j * tm + row_r + 1, window).astype(F32)
    pooled = total / cnt - pool_buf[HALO:n, :]
    y_pool = jnp.dot(pooled.astype(BF16), pool_w_ref[...], preferred_element_type=F32)
    ymix_ref[0, :, D_GROUP:2 * D_GROUP] = (y_pool * pool_scale).astype(BF16)


def _const_spec(shape):
    return pl.BlockSpec(shape, lambda *_: (0,) * len(shape))


def _layer_spec(stacked, layer, **kwargs):
    zeros = (0,) * (stacked.ndim - 1)
    return pl.BlockSpec((None,) + stacked.shape[1:], lambda *_: (layer,) + zeros, **kwargs)


def _mixer_in(layer, x, emb_g, emb_b, w_in, stacked, vectors):
    first = layer == 0
    assert D_MODEL == WEIGHT_CHUNK and D_IN % WEIGHT_CHUNK == 0
    B, S, _ = x.shape
    tm = TM_IN
    tok = lambda width: pl.BlockSpec((1, tm, width), lambda b, j: (b, j, 0))
    out_shape = [jax.ShapeDtypeStruct((B, S, D_GROUP), BF16),
                 jax.ShapeDtypeStruct((B, S, D_GROUP), BF16),
                 jax.ShapeDtypeStruct((B, S, D_GROUP), BF16),
                 jax.ShapeDtypeStruct((B, S, 2 * D_GROUP), BF16),
                 jax.ShapeDtypeStruct((B, S, 2 * D_GROUP), F32)]
    out_specs = [tok(D_GROUP), tok(D_GROUP), tok(D_GROUP), tok(2 * D_GROUP), tok(2 * D_GROUP)]
    if first:
        out_shape = [jax.ShapeDtypeStruct((B, S, D_MODEL), F32)] + out_shape
        out_specs = [tok(D_MODEL)] + out_specs
    return pl.pallas_call(
        functools.partial(_mixer_in_kernel, layer),
        grid=(B, S // tm),
        in_specs=([tok(D_MODEL), _const_spec(emb_g.shape), _const_spec(emb_b.shape),
                   pl.BlockSpec(memory_space=pl.ANY)]
                  + [_layer_spec(p, layer) for p in stacked] + [_const_spec(p.shape) for p in vectors]),
        out_specs=out_specs,
        out_shape=out_shape,
        scratch_shapes=[pltpu.VMEM((D_MODEL, D_IN), BF16),
                        pltpu.VMEM((2, WEIGHT_CHUNK, WEIGHT_CHUNK), F32),
                        pltpu.SemaphoreType.DMA((2,)),
                        pltpu.VMEM((HALO + tm, D_GROUP), F32),
                        pltpu.VMEM((SUBLANES - 1, HALO + tm - SUBLANES, D_GROUP), F32),
                        pltpu.VMEM((HALO + tm, D_GROUP), F32),
                        pltpu.VMEM((3, HALO + tm, D_GROUP), F32),
                        pltpu.VMEM((tm, D_GROUP), F32)],
        compiler_params=pltpu.CompilerParams(dimension_semantics=("arbitrary", "arbitrary"),
                                             vmem_limit_bytes=VMEM_LIMIT_BYTES),
        name="mixer_in",
    )(x, emb_g, emb_b, w_in, *stacked, *vectors)


def _diff_attn_kernel(layer, q_ref, k_ref, v_ref, table_ref, lq1_ref, lk1_ref, lq2_ref, lk2_ref, g_ref,
                      o_ref, bias_ref, vt_ref, qm_ref, s_ref, m_ref, a_ref, acc_ref):
    lam_init = 0.8 - 0.6 * math.exp(-0.3 * layer)
    c = pl.program_id(1)
    n_kv = k_ref.shape[1] // TK

    @pl.when((pl.program_id(0) == 0) & (c == 0))
    def _():
        for hd in range(N_HEADS):
            spread = jnp.broadcast_to(table_ref[hd:hd + 1, :], (TK, table_ref.shape[1]))
            sheared = pltpu.roll(spread, 0, axis=1, stride=1, stride_axis=0)
            bias_ref[0, :, hd * TQ:(hd + 1) * TQ] = sheared[:, 0:TQ]
            bias_ref[1, :, hd * TQ:(hd + 1) * TQ] = sheared[:, TK:TK + TQ]

    @pl.when(c == 0)
    def _():
        row = lax.broadcasted_iota(jnp.int32, (V_AUG - V_DIM, TK), 0)
        ones_rows = jnp.where(row == 0, 1.0, 0.0).astype(BF16)
        for jt in range(n_kv):
            vt = v_ref[0, jt * TK:(jt + 1) * TK, :].astype(F32).T
            for hd in range(N_HEADS):
                vt_ref[jt, hd, 0:V_DIM, :] = vt[hd * V_DIM:(hd + 1) * V_DIM, :].astype(BF16)
                vt_ref[jt, hd, V_DIM:V_AUG, :] = ones_rows

    st = types.SimpleNamespace()

    def use(parity):
        st.qm, st.s, st.m, st.a, st.acc = (r.at[parity] for r in (qm_ref, s_ref, m_ref, a_ref, acc_ref))

    def start_query_tile(q_rows):
        q = q_ref[0, q_rows, :].astype(F32)
        row = lax.broadcasted_iota(jnp.int32, (LANES, TQ), 0)
        for mp in range(2):
            qt = q[:, mp * LANES:(mp + 1) * LANES].T
            for hd in range(N_HEADS):
                keep = (row >= hd * QK_DIM) & (row < (hd + 1) * QK_DIM)
                st.qm[hd * 2 + mp] = jnp.where(keep, qt, 0.0).astype(BF16)
        st.m[1] = jnp.full(st.m.shape[1:], MASK_VALUE, F32)
        st.acc[...] = jnp.zeros(st.acc.shape, F32)

    def scores_into(slot, t, bias, hd):
        rows = pl.ds(pl.multiple_of(t * TK, TK), TK)
        for mp in range(2):
            blk = hd * 2 + mp
            cols = slice(blk * TQ, (blk + 1) * TQ)
            sc = jnp.dot(k_ref[0, rows, mp * LANES:(mp + 1) * LANES], st.qm[blk],
                         preferred_element_type=F32)
            if bias is not None:
                sc = sc + bias_ref[bias, :, hd * TQ:(hd + 1) * TQ]
            st.s[slot, :, cols] = sc
            m_old = st.m[1 - slot, :, cols]
            m_new = jnp.maximum(m_old, jnp.max(sc, axis=0, keepdims=True))
            st.m[slot, :, cols] = m_new
            st.a[slot, :, cols] = jnp.exp2(m_old - m_new)

    def values_from(slot, t, hd):
        cols = slice(hd * 2 * TQ, (hd + 1) * 2 * TQ)
        p = jnp.exp2(st.s[slot, :, cols] - st.m[slot, :, cols]).astype(BF16)
        pv = jnp.dot(vt_ref[t, hd], p, preferred_element_type=F32)
        st.acc[hd] = st.acc[hd] * st.a[slot, :, cols] + pv

    def stage(score_args, value_args):
        for hd in range(N_HEADS):
            if score_args is not None:
                scores_into(*score_args, hd)
            if value_args is not None:
                values_from(*value_args, hd)

    def last_values_and_finalize(slot, q_rows):
        row = slice(layer, layer + 1)
        lam = (jnp.exp(jnp.sum(lq1_ref[row, :] * lk1_ref[row, :], keepdims=True))
               - jnp.exp(jnp.sum(lq2_ref[row, :] * lk2_ref[row, :], keepdims=True)) + lam_init)
        outs = []
        for hd in range(N_HEADS):
            values_from(slot, 0, hd)
            a = st.acc[hd]
            o1 = a[0:V_DIM, 0:TQ] / a[V_DIM:V_DIM + 1, 0:TQ]
            o2 = a[0:V_DIM, TQ:2 * TQ] / a[V_DIM:V_DIM + 1, TQ:2 * TQ]
            o = o1 - lam * o2
            ms = jnp.mean(o * o, axis=0, keepdims=True)
            outs.append(o * lax.rsqrt(ms + LN_EPS) * (1.0 - lam_init))
        o_ref[0, q_rows, :] = (jnp.concatenate(outs, axis=0).T * g_ref[row, :]).astype(BF16)

    def first_two_steps(i):
        stage((0, i, 0), None)
        stage((1, i - 1, 1), (0, i))

    def far_pairs(i, n_pairs):
        def far_pair(k2, carry):
            t = i - 2 - 2 * k2
            stage((0, t, None), (1, t + 1))
            stage((1, t - 1, None), (0, t))
            return carry
        lax.fori_loop(0, n_pairs, far_pair, 0)

    tile_rows = [slice(g * TQ, (g + 1) * TQ) for g in range(Q_PER_STEP)]
    tile_index = [Q_PER_STEP * c + g for g in range(Q_PER_STEP)]

    def finish_and_start_next(g):
        use((g + 1) % 2)
        start_query_tile(tile_rows[g + 1])
        first_two_steps(tile_index[g + 1])
        use(g % 2)
        last_values_and_finalize(g % 2, tile_rows[g])

    use(0)
    start_query_tile(tile_rows[0])

    @pl.when(c == 0)
    def _():
        use(0)
        stage((0, 0, 0), None)
        finish_and_start_next(0)

    @pl.when(c >= 1)
    def _():
        use(0)
        first_two_steps(tile_index[0])

    use(0)
    far_pairs(tile_index[0], jnp.maximum(Q_PER_STEP // 2 * c - 1, 0))

    @pl.when(c >= 1)
    def _():
        use(0)
        stage((0, 0, None), (1, 1))
        finish_and_start_next(0)

    for g in range(1, Q_PER_STEP):
        use(g % 2)
        far_pairs(tile_index[g], (tile_index[g] - 1) // 2)
        if g % 2 == 0:
            stage((0, 0, None), (1, 1))
        if g + 1 < Q_PER_STEP:
            finish_and_start_next(g)
        else:
            last_values_and_finalize(g % 2, tile_rows[g])


def _diff_attn(layer, q, k, v, table, vectors):
    B, S, _ = q.shape
    n_kv = S // TK
    return pl.pallas_call(
        functools.partial(_diff_attn_kernel, layer),
        grid=(B, S // (Q_PER_STEP * TQ)),
        in_specs=[pl.BlockSpec((1, Q_PER_STEP * TQ, D_GROUP), lambda b, c: (b, c, 0)),
                  pl.BlockSpec((1, S, D_GROUP), lambda b, c: (b, 0, 0)),
                  pl.BlockSpec((1, S, D_GROUP), lambda b, c: (b, 0, 0)),
                  _const_spec(table.shape)] + [_const_spec(p.shape) for p in vectors],
        out_specs=pl.BlockSpec((1, Q_PER_STEP * TQ, D_GROUP), lambda b, c: (b, c, 0)),
        out_shape=jax.ShapeDtypeStruct((B, S, D_GROUP), BF16),
        scratch_shapes=[pltpu.VMEM((2, TK, N_HEADS * TQ), F32),
                        pltpu.VMEM((n_kv, N_HEADS, V_AUG, TK), BF16),
                        pltpu.VMEM((2, N_MAPS, LANES, TQ), BF16),
                        pltpu.VMEM((2, 2, TK, N_MAPS * TQ), F32),
                        pltpu.VMEM((2, 2, 1, N_MAPS * TQ), F32),
                        pltpu.VMEM((2, 2, 1, N_MAPS * TQ), F32),
                        pltpu.VMEM((2, N_HEADS, V_AUG, 2 * TQ), F32)],
        compiler_params=pltpu.CompilerParams(dimension_semantics=("arbitrary", "arbitrary"),
                                             vmem_limit_bytes=VMEM_LIMIT_BYTES),
        name="diff_attn",
    )(q, k, v, table, *vectors)


def _spatial_gating(gate, sgln_g, sgln_b, sg_w, sg_bias):
    rows = gate.shape[0]
    gu = _gelu_tanh(gate[:, 0:D_GROUP])
    gv = _layer_norm(_gelu_tanh(gate[:, D_GROUP:2 * D_GROUP]), sgln_g, sgln_b).astype(BF16)
    lane = lax.broadcasted_iota(jnp.int32, (CHUNK, D_GROUP), 1)
    out = []
    for c in range(rows // CHUNK):
        vc = gv[c * CHUNK:(c + 1) * CHUNK, :]
        stacked = jnp.concatenate(
            [jnp.where((lane >= hd * SG_HEAD_DIM) & (lane < (hd + 1) * SG_HEAD_DIM), vc,
                       jnp.zeros_like(vc)) for hd in range(N_SG_HEADS)], axis=0)
        mixed = jnp.dot(sg_w, stacked, preferred_element_type=F32) + sg_bias
        out.append((gu[c * CHUNK:(c + 1) * CHUNK, :] * mixed).astype(BF16))
    return jnp.concatenate(out, axis=0)


def _mixer_out_kernel(layer, h_ref, ymix_ref, ydiff_ref, gate_ref, w_out_hbm, w1_hbm, w2_hbm, sg_w_ref,
                      sg_bias_ref, sgln_g_ref, sgln_b_ref, ln1_g_ref, ln1_b_ref, ln2_g_ref, ln2_b_ref,
                      o_ref, w_out_ref, w1_ref, w2_ref, stage_ref, sem):
    @pl.when(pl.program_id(0) == 0)
    def _():
        side = stage_ref.shape[1]
        cols = lambda c: pl.ds(c * side, side)
        chunks = [(w_out_hbm.at[layer], w_out_ref)]
        chunks += [(w1_hbm.at[layer, :, cols(c)], w1_ref.at[:, cols(c)]) for c in range(D_FF // side)]
        chunks += [(w2_hbm.at[layer, cols(c), :], w2_ref.at[cols(c), :]) for c in range(D_FF // side)]
        _stream_cast(chunks, stage_ref, sem)

    ln1_g, ln1_b, ln2_g, ln2_b, sgln_g, sgln_b = (
        r[layer:layer + 1, :] for r in (ln1_g_ref, ln1_b_ref, ln2_g_ref, ln2_b_ref, sgln_g_ref, sgln_b_ref))
    tm = h_ref.shape[0]
    parts = [slice(r, r + OUT_ROWS) for r in range(0, tm, OUT_ROWS)]
    mix = []
    for rows in parts:
        ymix = ymix_ref[rows, :]
        mix.append(jnp.dot(ymix[:, 0:D_GROUP], w_out_ref[0:D_GROUP, :], preferred_element_type=F32)
                   + jnp.dot(ydiff_ref[rows, :], w_out_ref[D_GROUP:2 * D_GROUP, :],
                             preferred_element_type=F32)
                   + jnp.dot(ymix[:, D_GROUP:2 * D_GROUP], w_out_ref[3 * D_GROUP:4 * D_GROUP, :],
                             preferred_element_type=F32))
    y_sg = [_spatial_gating(gate_ref[rows, :], sgln_g, sgln_b, sg_w_ref[...], sg_bias_ref[...])
            for rows in parts]
    mix = [m + jnp.dot(sg, w_out_ref[2 * D_GROUP:3 * D_GROUP, :], preferred_element_type=F32)
           for m, sg in zip(mix, y_sg)]
    h1 = [_layer_norm(ALPHA * h_ref[rows, :] + m, ln1_g, ln1_b) for rows, m in zip(parts, mix)]
    act = []
    for x in h1:
        a = jnp.maximum(jnp.dot(x.astype(BF16), w1_ref[...], preferred_element_type=F32), 0.0)
        act.append((a * a).astype(BF16))
    ff = [jnp.dot(a, w2_ref[...], preferred_element_type=F32) for a in act]
    for rows, x, f in zip(parts, h1, ff):
        o_ref[rows, :] = _layer_norm(ALPHA * x + f, ln2_g, ln2_b)


def _mixer_out(layer, h, ymix, ydiff, gate, weights, stacked, vectors):
    T = h.shape[0]
    tm = TM_OUT
    tok = lambda width: pl.BlockSpec((tm, width), lambda t: (t, 0))
    assert D_MODEL == WEIGHT_CHUNK and D_FF % WEIGHT_CHUNK == 0 and OUT_ROWS % CHUNK == 0
    return pl.pallas_call(
        functools.partial(_mixer_out_kernel, layer),
        grid=(T // tm,),
        in_specs=([tok(D_MODEL), tok(2 * D_GROUP), tok(D_GROUP), tok(2 * D_GROUP)]
                  + [pl.BlockSpec(memory_space=pl.ANY) for _ in weights]
                  + [_layer_spec(p, layer) for p in stacked]
                  + [_const_spec(p.shape) for p in vectors]),
        out_specs=tok(D_MODEL),
        out_shape=jax.ShapeDtypeStruct((T, D_MODEL), F32),
        scratch_shapes=[pltpu.VMEM((D_MODEL, D_MODEL), BF16),
                        pltpu.VMEM((D_MODEL, D_FF), BF16),
                        pltpu.VMEM((D_FF, D_MODEL), BF16),
                        pltpu.VMEM((2, WEIGHT_CHUNK, WEIGHT_CHUNK), F32),
                        pltpu.SemaphoreType.DMA((2,))],
        compiler_params=pltpu.CompilerParams(dimension_semantics=("arbitrary",),
                                             vmem_limit_bytes=VMEM_LIMIT_BYTES),
        name="mixer_out",
    )(h, ymix, ydiff, gate, *weights, *stacked, *vectors)


def _t5_bucket(n):
    max_exact = N_BUCKETS // 2
    large = max_exact + (jnp.log(jnp.maximum(n, 1).astype(F32) / max_exact)
                         / math.log(MAX_DISTANCE / max_exact) * (N_BUCKETS - max_exact)).astype(jnp.int32)
    large = jnp.minimum(large, N_BUCKETS - 1)
    return jnp.where(n < max_exact, n, large)


def _bias_table(rel_bias):
    assert TK >= MAX_DISTANCE, "tiles two or more behind the diagonal must see one constant bias"
    dist = jnp.arange(TQ + TK)
    onehot = (_t5_bucket(dist)[:, None] == jnp.arange(N_BUCKETS)[None, :]).astype(F32)
    rb = rel_bias.astype(F32)
    vals = jnp.sum(onehot[:, :, None] * (rb - rb[N_BUCKETS - 1])[None, :, :], axis=1) * LOG2E
    return jnp.concatenate([vals, jnp.full((TQ + TK, N_HEADS), MASK_VALUE, F32)], axis=0).T


def _block_diag(w):
    n_layers, g, c, d = w.shape
    out = jnp.zeros((n_layers, g * c, g * d), w.dtype)
    for i in range(g):
        out = out.at[:, i * c:(i + 1) * c, i * d:(i + 1) * d].set(w[:, i])
    return out


def kernel(x, emb_ln_g, emb_ln_b, rel_bias, w_in, conv_w, conv_b, conv_ln_g, conv_ln_b, conv_pw_w,
           conv_pw_b, lam_q1, lam_k1, lam_q2, lam_k2, diff_norm_g, sg_ln_g, sg_ln_b, sg_w, sg_b, pool_w,
           pool_scale, w_out, ln1_g, ln1_b, w_mlp1, w_mlp2, ln2_g, ln2_b):
    B, S, _ = x.shape
    assert S % TM_IN == 0 and (B * S) % TM_OUT == 0 and TQ == TK
    assert S % (Q_PER_STEP * TQ) == 0 and Q_PER_STEP % 2 == 0
    assert TM_OUT % OUT_ROWS == 0
    assert all(w.dtype == F32 for w in (w_in, w_out, w_mlp1, w_mlp2)), "weights are staged as f32"
    small = [conv_b, conv_ln_g, conv_ln_b, conv_pw_b, sg_ln_g, sg_ln_b, pool_scale, lam_q1, lam_k1,
             lam_q2, lam_k2, diff_norm_g, ln1_g, ln1_b, ln2_g, ln2_b]
    assert all(p.dtype == F32 and p.shape[0] == DEPTH and p.ndim == 2 for p in small)
    tril = jnp.tril(jnp.ones((CHUNK, CHUNK), F32))
    sg_w_cat = jnp.transpose(sg_w * tril, (0, 2, 1, 3)).reshape(DEPTH, CHUNK, N_SG_HEADS * CHUNK)
    sg_bias = jnp.repeat(jnp.swapaxes(sg_b, 1, 2), SG_HEAD_DIM, axis=2).astype(F32)
    in_matrices = [conv_w.astype(F32), conv_pw_w.astype(BF16), _block_diag(pool_w).astype(BF16)]
    in_vectors = [conv_b, conv_ln_g, conv_ln_b, conv_pw_b, pool_scale]
    attn_vectors = [lam_q1, lam_k1, lam_q2, lam_k2, jnp.tile(diff_norm_g, (1, N_HEADS))]
    out_weights = [w_out, w_mlp1, w_mlp2]
    out_matrices = [sg_w_cat.astype(BF16), sg_bias]
    out_vectors = [sg_ln_g, sg_ln_b, ln1_g, ln1_b, ln2_g, ln2_b]
    table = _bias_table(rel_bias)
    emb_g, emb_b = emb_ln_g.reshape(1, -1).astype(F32), emb_ln_b.reshape(1, -1).astype(F32)

    h = x
    for l in range(DEPTH):
        outs = _mixer_in(l, h, emb_g, emb_b, w_in, in_matrices, in_vectors)
        if l == 0:
            h, q, k, v, ymix, gate = outs
        else:
            q, k, v, ymix, gate = outs
        ydiff = _diff_attn(l, q, k, v, table, attn_vectors)
        h = _mixer_out(l, h.reshape(B * S, D_MODEL), ymix.reshape(B * S, 2 * D_GROUP),
                       ydiff.reshape(B * S, D_GROUP), gate.reshape(B * S, 2 * D_GROUP), out_weights,
                       out_matrices, out_vectors).reshape(B, S, D_MODEL)
    return h
```

```python
import functools
import math
import types

import jax
import jax.numpy as jnp
from jax import lax
from jax.experimental import pallas as pl
from jax.experimental.pallas import tpu as pltpu

F32 = jnp.float32
BF16 = jnp.bfloat16

D_MODEL = 1024
DEPTH = 2
D_GROUP = 256
CONV_WIDTH = 31
N_HEADS = 4
QK_DIM = 32
V_DIM = 64
N_SG_HEADS = 4
SG_HEAD_DIM = 64
CHUNK = 128
POOL_WINDOWS = (2, 4, 8, 16)
POOL_GROUP_DIM = 64
D_FF = 4096
N_BUCKETS = 32
MAX_DISTANCE = 128
LN_EPS = 1e-5
ALPHA = (2.0 * DEPTH) ** 0.25
D_IN = 2048
O_A, O_Q, O_K, O_V, O_U, O_SV, O_P = 0, 512, 768, 1024, 1280, 1536, 1792

LANES = 128
SUBLANES = 8
VMEM_LIMIT_BYTES = 56 * 1024 * 1024

TM_IN = 1024
TM_OUT = 512
OUT_ROWS = 256
WEIGHT_CHUNK = 1024
TQ = 256
TK = 256
Q_PER_STEP = 4
HALO = 32
ROW_CHUNK = 64
V_AUG = 80
N_MAPS = 2 * N_HEADS
MASK_VALUE = -1e30
LOG2E = math.log2(math.e)
Q_SCALE = (QK_DIM ** -0.5) * LOG2E


def _layer_norm(x, g, b):
    mu = jnp.mean(x, axis=-1, keepdims=True)
    xc = x - mu
    var = jnp.mean(xc * xc, axis=-1, keepdims=True)
    return xc * lax.rsqrt(var + LN_EPS) * g + b


def _gelu_tanh(x):
    c = math.sqrt(2.0 / math.pi)
    return 0.5 * x * (1.0 + jnp.tanh(c * (x + 0.044715 * (x * x * x))))


def _sigmoid(x):
    return 1.0 / (1.0 + jnp.exp(-x))


def _runtime_zero_bits(v):
    bits = pltpu.bitcast(v, jnp.int32)
    return lax.shift_right_logical(lax.shift_right_logical(bits, 16), 16)


def _runtime_zero(v):
    return _runtime_zero_bits(v).astype(F32)


def _stream_cast(chunks, stage_ref, sem):
    def copy(c):
        return pltpu.make_async_copy(chunks[c][0], stage_ref.at[c % 2], sem.at[c % 2])

    copy(0).start()
    for c, (_, dst) in enumerate(chunks):
        if c + 1 < len(chunks):
            copy(c + 1).start()
        copy(c).wait()
        dst[...] = stage_ref[c % 2].astype(BF16)


def _mixer_in_kernel(layer, x_ref, emb_g_ref, emb_b_ref, w_in_hbm, conv_w_ref, pw_w_ref, pool_w_ref,
                     *rest):
    first = layer == 0
    conv_b, cln_g, cln_b, pw_b, pool_scale = (r[layer:layer + 1, :] for r in rest[:5])
    rest = rest[5:]
    if first:
        h_ref, *rest = rest
    (q_ref, k_ref, v_ref, ymix_ref, gate_ref, w_in_ref, stage_ref, sem, glu_buf, shift_buf, pool_buf,
     win_buf, conv_buf) = rest
    tm = x_ref.shape[1]
    j = pl.program_id(1)

    @pl.when((pl.program_id(0) == 0) & (j == 0))
    def _():
        side = stage_ref.shape[1]
        cols = lambda c: pl.ds(c * side, side)
        _stream_cast([(w_in_hbm.at[layer, :, cols(c)], w_in_ref.at[:, cols(c)])
                      for c in range(D_IN // side)], stage_ref, sem)

    @pl.when(j == 0)
    def _():
        glu_buf[0:HALO, :] = jnp.zeros((HALO, D_GROUP), F32)
        pool_buf[0:HALO, :] = jnp.zeros((HALO, D_GROUP), F32)

    @pl.when(j > 0)
    def _():
        glu_buf[0:HALO, :] = glu_buf[tm:tm + HALO, :]
        pool_buf[0:HALO, :] = pool_buf[tm:tm + HALO, :]

    x = x_ref[0]
    if first:
        h = _layer_norm(x, emb_g_ref[...], emb_b_ref[...])
        h_ref[0] = h
    else:
        h = x
    hb = h.astype(BF16)

    def first_vreg(value):
        return value[0:SUBLANES, 0:LANES]

    def proj(lo, hi, after=None):
        w = w_in_ref[:, lo:hi]
        if after is not None:
            bits = pltpu.bitcast(w, jnp.int32)
            zero = _runtime_zero_bits(after)
            bits = bits | jnp.tile(zero, (bits.shape[0] // SUBLANES, bits.shape[1] // LANES))
            w = pltpu.bitcast(bits, BF16)
        return jnp.dot(hb, w, preferred_element_type=F32)

    n = HALO + tm
    n_shift = n - SUBLANES
    quarter = tm // 4

    def conv_rows(r_lo, r_hi, after):
        for r0 in range(r_lo, r_hi, ROW_CHUNK):
            acc = jnp.broadcast_to(conv_b, (ROW_CHUNK, D_GROUP))
            if r0 == r_lo:
                acc = acc + jnp.tile(_runtime_zero(after), (ROW_CHUNK // SUBLANES, D_GROUP // LANES))
            for tap in range(CONV_WIDTH):
                off = HALO + r0 - (CONV_WIDTH - 1) + tap
                base, s = off - off % SUBLANES, off % SUBLANES
                if s == 0:
                    rows = glu_buf[base:base + ROW_CHUNK, :]
                else:
                    assert base + ROW_CHUNK <= n_shift
                    rows = shift_buf[s - 1, base:base + ROW_CHUNK, :]
                acc = acc + conv_w_ref[tap:tap + 1, :] * rows
            conv_buf[r0:r0 + ROW_CHUNK, :] = acc
        return first_vreg(acc)

    za = proj(O_A, O_Q)
    glu_buf[HALO:n, :] = za[:, :D_GROUP] * _sigmoid(za[:, D_GROUP:])
    q_ref[0] = (proj(O_Q, O_K) * Q_SCALE).astype(BF16)
    for s in range(1, SUBLANES):
        shift_buf[s - 1] = glu_buf[s:s + n_shift, :]

    zk = proj(O_K, O_V)
    k_ref[0] = zk.astype(BF16)
    c0 = conv_rows(0, quarter, first_vreg(zk))
    zv = proj(O_V, O_U)
    v_ref[0] = zv.astype(BF16)
    c1 = conv_rows(quarter, 2 * quarter, first_vreg(zv))
    zu = proj(O_U, O_SV, after=c0)
    gate_ref[0, :, 0:D_GROUP] = zu
    c2 = conv_rows(2 * quarter, 3 * quarter, first_vreg(zu))
    zsv = proj(O_SV, O_P, after=c1)
    gate_ref[0, :, D_GROUP:2 * D_GROUP] = zsv
    conv_rows(3 * quarter, tm, first_vreg(zsv))
    pool_buf[HALO:n, :] = proj(O_P, D_IN, after=c2)

    hc = _layer_norm(conv_buf[...], cln_g, cln_b)
    hc = hc * _sigmoid(hc)
    y_conv = jnp.dot(hc.astype(BF16), pw_w_ref[...], preferred_element_type=F32) + pw_b
    ymix_ref[0, :, 0:D_GROUP] = y_conv.astype(BF16)

    assert POOL_WINDOWS == (2, 4, 8, 16) and HALO == 32
    win_buf[0, 8:n, :] = pool_buf[8:n, :] + pool_buf[7:n - 1, :]
    win_buf[1, 16:n, :] = win_buf[0, 16:n, :] + win_buf[0, 14:n - 2, :]
    win_buf[2, 24:n, :] = win_buf[1, 24:n, :] + win_buf[1, 20:n - 4, :]
    sum16 = win_buf[2, HALO:n, :] + win_buf[2, HALO - 8:n - 8, :]
    lane_r = lax.broadcasted_iota(jnp.int32, (tm, D_GROUP), 1)
    row_r = lax.broadcasted_iota(jnp.int32, (tm, D_GROUP), 0)
    group = lane_r // POOL_GROUP_DIM
    window = jnp.where(group == 0, POOL_WINDOWS[0],
                       jnp.where(group == 1, POOL_WINDOWS[1],
                                 jnp.where(group == 2, POOL_WINDOWS[2], POOL_WINDOWS[3])))
    total = jnp.where(group == 0, win_buf[0, HALO:n, :],
                      jnp.where(group == 1, win_buf[1, HALO:n, :],
                                jnp.where(group == 2, win_buf[2, HALO:n, :], sum16)))
    cnt = jnp.minimum(j * tm + row_r + 1, window).astype(F32)
    pooled = total / cnt - pool_buf[HALO:n, :]
    y_pool = jnp.dot(pooled.astype(BF16), pool_w_ref[...], preferred_element_type=F32)
    ymix_ref[0, :, D_GROUP:2 * D_GROUP] = (y_pool * pool_scale).astype(BF16)


def _const_spec(shape):
    return pl.BlockSpec(shape, lambda *_: (0,) * len(shape))


def _layer_spec(stacked, layer, **kwargs):
    zeros = (0,) * (stacked.ndim - 1)
    return pl.BlockSpec((None,) + stacked.shape[1:], lambda *_: (layer,) + zeros, **kwargs)


def _mixer_in(layer, x, emb_g, emb_b, w_in, stacked, vectors):
    first = layer == 0
    assert D_MODEL == WEIGHT_CHUNK and D_IN % WEIGHT_CHUNK == 0
    B, S, _ = x.shape
    tm = TM_IN
    tok = lambda width: pl.BlockSpec((1, tm, width), lambda b, j: (b, j, 0))
    out_shape = [jax.ShapeDtypeStruct((B, S, D_GROUP), BF16),
                 jax.ShapeDtypeStruct((B, S, D_GROUP), BF16),
                 jax.ShapeDtypeStruct((B, S, D_GROUP), BF16),
                 jax.ShapeDtypeStruct((B, S, 2 * D_GROUP), BF16),
                 jax.ShapeDtypeStruct((B, S, 2 * D_GROUP), F32)]
    out_specs = [tok(D_GROUP), tok(D_GROUP), tok(D_GROUP), tok(2 * D_GROUP), tok(2 * D_GROUP)]
    if first:
        out_shape = [jax.ShapeDtypeStruct((B, S, D_MODEL), F32)] + out_shape
        out_specs = [tok(D_MODEL)] + out_specs
    return pl.pallas_call(
        functools.partial(_mixer_in_kernel, layer),
        grid=(B, S // tm),
        in_specs=([tok(D_MODEL), _const_spec(emb_g.shape), _const_spec(emb_b.shape),
                   pl.BlockSpec(memory_space=pl.ANY)]
                  + [_layer_spec(p, layer) for p in stacked] + [_const_spec(p.shape) for p in vectors]),
        out_specs=out_specs,
        out_shape=out_shape,
        scratch_shapes=[pltpu.VMEM((D_MODEL, D_IN), BF16),
                        pltpu.VMEM((2, WEIGHT_CHUNK, WEIGHT_CHUNK), F32),
                        pltpu.SemaphoreType.DMA((2,)),
                        pltpu.VMEM((HALO + tm, D_GROUP), F32),
                        pltpu.VMEM((SUBLANES - 1, HALO + tm - SUBLANES, D_GROUP), F32),
                        pltpu.VMEM((HALO + tm, D_GROUP), F32),
                        pltpu.VMEM((3, HALO + tm, D_GROUP), F32),
                        pltpu.VMEM((tm, D_GROUP), F32)],
        compiler_params=pltpu.CompilerParams(dimension_semantics=("arbitrary", "arbitrary"),
                                             vmem_limit_bytes=VMEM_LIMIT_BYTES),
        name="mixer_in",
    )(x, emb_g, emb_b, w_in, *stacked, *vectors)


def _diff_attn_kernel(layer, q_ref, k_hbm, v_ref, table_ref, lq1_ref, lk1_ref, lq2_ref, lk2_ref, g_ref,
                      o_ref, bias_ref, vt_ref, qm_ref, s_ref, m_ref, a_ref, acc_ref, k_buf, k_sem):
    lam_init = 0.8 - 0.6 * math.exp(-0.3 * layer)
    c = pl.program_id(1)
    n_kv = k_buf.shape[0] // TK
    k_copy = pltpu.make_async_copy(k_hbm.at[pl.program_id(0)], k_buf, k_sem.at[0])

    @pl.when((pl.program_id(0) == 0) & (c == 0))
    def _():
        for hd in range(N_HEADS):
            spread = jnp.broadcast_to(table_ref[hd:hd + 1, :], (TK, table_ref.shape[1]))
            sheared = pltpu.roll(spread, 0, axis=1, stride=1, stride_axis=0)
            bias_ref[0, :, hd * TQ:(hd + 1) * TQ] = sheared[:, 0:TQ]
            bias_ref[1, :, hd * TQ:(hd + 1) * TQ] = sheared[:, TK:TK + TQ]

    @pl.when(c == 0)
    def _():
        k_copy.start()
        row = lax.broadcasted_iota(jnp.int32, (V_AUG - V_DIM, TK), 0)
        ones_rows = jnp.where(row == 0, 1.0, 0.0).astype(BF16)
        for jt in range(n_kv):
            vt = v_ref[0, jt * TK:(jt + 1) * TK, :].astype(F32).T
            for hd in range(N_HEADS):
                vt_ref[jt, hd, 0:V_DIM, :] = vt[hd * V_DIM:(hd + 1) * V_DIM, :].astype(BF16)
                vt_ref[jt, hd, V_DIM:V_AUG, :] = ones_rows
        k_copy.wait()

    st = types.SimpleNamespace()

    def use(parity):
        st.qm, st.s, st.m, st.a, st.acc = (r.at[parity] for r in (qm_ref, s_ref, m_ref, a_ref, acc_ref))

    def start_query_tile(q_rows):
        q = q_ref[0, q_rows, :].astype(F32)
        row = lax.broadcasted_iota(jnp.int32, (LANES, TQ), 0)
        for mp in range(2):
            qt = q[:, mp * LANES:(mp + 1) * LANES].T
            for hd in range(N_HEADS):
                keep = (row >= hd * QK_DIM) & (row < (hd + 1) * QK_DIM)
                st.qm[hd * 2 + mp] = jnp.where(keep, qt, 0.0).astype(BF16)
        st.m[1] = jnp.full(st.m.shape[1:], MASK_VALUE, F32)
        st.acc[...] = jnp.zeros(st.acc.shape, F32)

    def scores_into(slot, t, bias, hd):
        rows = pl.ds(pl.multiple_of(t * TK, TK), TK)
        for mp in range(2):
            blk = hd * 2 + mp
            cols = slice(blk * TQ, (blk + 1) * TQ)
            sc = jnp.dot(k_buf[rows, mp * LANES:(mp + 1) * LANES], st.qm[blk],
                         preferred_element_type=F32)
            if bias is not None:
                sc = sc + bias_ref[bias, :, hd * TQ:(hd + 1) * TQ]
            st.s[slot, :, cols] = sc
            m_old = st.m[1 - slot, :, cols]
            m_new = jnp.maximum(m_old, jnp.max(sc, axis=0, keepdims=True))
            st.m[slot, :, cols] = m_new
            st.a[slot, :, cols] = jnp.exp2(m_old - m_new)

    def values_from(slot, t, hd):
        cols = slice(hd * 2 * TQ, (hd + 1) * 2 * TQ)
        p = jnp.exp2(st.s[slot, :, cols] - st.m[slot, :, cols]).astype(BF16)
        pv = jnp.dot(vt_ref[t, hd], p, preferred_element_type=F32)
        st.acc[hd] = st.acc[hd] * st.a[slot, :, cols] + pv

    def stage(score_args, value_args):
        for hd in range(N_HEADS):
            if score_args is not None:
                scores_into(*score_args, hd)
            if value_args is not None:
                values_from(*value_args, hd)

    def last_values_and_finalize(slot, q_rows):
        row = slice(layer, layer + 1)
        lam = (jnp.exp(jnp.sum(lq1_ref[row, :] * lk1_ref[row, :], keepdims=True))
               - jnp.exp(jnp.sum(lq2_ref[row, :] * lk2_ref[row, :], keepdims=True)) + lam_init)
        outs = []
        for hd in range(N_HEADS):
            values_from(slot, 0, hd)
            a = st.acc[hd]
            o1 = a[0:V_DIM, 0:TQ] / a[V_DIM:V_DIM + 1, 0:TQ]
            o2 = a[0:V_DIM, TQ:2 * TQ] / a[V_DIM:V_DIM + 1, TQ:2 * TQ]
            o = o1 - lam * o2
            ms = jnp.mean(o * o, axis=0, keepdims=True)
            outs.append(o * lax.rsqrt(ms + LN_EPS) * (1.0 - lam_init))
        o_ref[0, q_rows, :] = (jnp.concatenate(outs, axis=0).T * g_ref[row, :]).astype(BF16)

    def first_two_steps(i):
        stage((0, i, 0), None)
        stage((1, i - 1, 1), (0, i))

    def far_pairs(i, n_pairs):
        def far_pair(k2, carry):
            t = i - 2 - 2 * k2
            stage((0, t, None), (1, t + 1))
            stage((1, t - 1, None), (0, t))
            return carry
        lax.fori_loop(0, n_pairs, far_pair, 0)

    tile_rows = [slice(g * TQ, (g + 1) * TQ) for g in range(Q_PER_STEP)]
    tile_index = [Q_PER_STEP * c + g for g in range(Q_PER_STEP)]

    def finish_and_start_next(g):
        use((g + 1) % 2)
        start_query_tile(tile_rows[g + 1])
        first_two_steps(tile_index[g + 1])
        use(g % 2)
        last_values_and_finalize(g % 2, tile_rows[g])

    use(0)
    start_query_tile(tile_rows[0])

    @pl.when(c == 0)
    def _():
        use(0)
        stage((0, 0, 0), None)
        finish_and_start_next(0)

    @pl.when(c >= 1)
    def _():
        use(0)
        first_two_steps(tile_index[0])

    use(0)
    far_pairs(tile_index[0], jnp.maximum(Q_PER_STEP // 2 * c - 1, 0))

    @pl.when(c >= 1)
    def _():
        use(0)
        stage((0, 0, None), (1, 1))
        finish_and_start_next(0)

    for g in range(1, Q_PER_STEP):
        use(g % 2)
        far_pairs(tile_index[g], (tile_index[g] - 1) // 2)
        if g % 2 == 0:
            stage((0, 0, None), (1, 1))
        if g + 1 < Q_PER_STEP:
            finish_and_start_next(g)
        else:
            last_values_and_finalize(g % 2, tile_rows[g])


def _diff_attn(layer, q, k, v, table, vectors):
    B, S, _ = q.shape
    n_kv = S // TK
    return pl.pallas_call(
        functools.partial(_diff_attn_kernel, layer),
        grid=(B, S // (Q_PER_STEP * TQ)),
        in_specs=[pl.BlockSpec((1, Q_PER_STEP * TQ, D_GROUP), lambda b, c: (b, c, 0)),
                  pl.BlockSpec(memory_space=pl.ANY),
                  pl.BlockSpec((1, S, D_GROUP), lambda b, c: (b, 0, 0)),
                  _const_spec(table.shape)] + [_const_spec(p.shape) for p in vectors],
        out_specs=pl.BlockSpec((1, Q_PER_STEP * TQ, D_GROUP), lambda b, c: (b, c, 0)),
        out_shape=jax.ShapeDtypeStruct((B, S, D_GROUP), BF16),
        scratch_shapes=[pltpu.VMEM((2, TK, N_HEADS * TQ), F32),
                        pltpu.VMEM((n_kv, N_HEADS, V_AUG, TK), BF16),
                        pltpu.VMEM((2, N_MAPS, LANES, TQ), BF16),
                        pltpu.VMEM((2, 2, TK, N_MAPS * TQ), F32),
                        pltpu.VMEM((2, 2, 1, N_MAPS * TQ), F32),
                        pltpu.VMEM((2, 2, 1, N_MAPS * TQ), F32),
                        pltpu.VMEM((2, N_HEADS, V_AUG, 2 * TQ), F32),
                        pltpu.VMEM((S, D_GROUP), BF16),
                        pltpu.SemaphoreType.DMA((1,))],
        compiler_params=pltpu.CompilerParams(dimension_semantics=("arbitrary", "arbitrary"),
                                             vmem_limit_bytes=VMEM_LIMIT_BYTES),
        name="diff_attn",
    )(q, k, v, table, *vectors)


def _spatial_gating(gate, sgln_g, sgln_b, sg_w, sg_bias):
    rows = gate.shape[0]
    gu = _gelu_tanh(gate[:, 0:D_GROUP])
    gv = _layer_norm(_gelu_tanh(gate[:, D_GROUP:2 * D_GROUP]), sgln_g, sgln_b).astype(BF16)
    lane = lax.broadcasted_iota(jnp.int32, (CHUNK, D_GROUP), 1)
    out = []
    for c in range(rows // CHUNK):
        vc = gv[c * CHUNK:(c + 1) * CHUNK, :]
        stacked = jnp.concatenate(
            [jnp.where((lane >= hd * SG_HEAD_DIM) & (lane < (hd + 1) * SG_HEAD_DIM), vc,
                       jnp.zeros_like(vc)) for hd in range(N_SG_HEADS)], axis=0)
        mixed = jnp.dot(sg_w, stacked, preferred_element_type=F32) + sg_bias
        out.append((gu[c * CHUNK:(c + 1) * CHUNK, :] * mixed).astype(BF16))
    return jnp.concatenate(out, axis=0)


def _mixer_out_kernel(layer, h_ref, ymix_ref, ydiff_ref, gate_ref, w_out_hbm, w1_hbm, w2_hbm, sg_w_ref,
                      sg_bias_ref, sgln_g_ref, sgln_b_ref, ln1_g_ref, ln1_b_ref, ln2_g_ref, ln2_b_ref,
                      o_ref, w_out_ref, w1_ref, w2_ref, stage_ref, sem, pre_buf):
    t = pl.program_id(0)
    n_tiles = pl.num_programs(0) - 1

    @pl.when(t == 0)
    def _():
        side = stage_ref.shape[1]
        cols = lambda c: pl.ds(c * side, side)
        chunks = [(w_out_hbm.at[layer], w_out_ref)]
        chunks += [(w1_hbm.at[layer, :, cols(c)], w1_ref.at[:, cols(c)]) for c in range(D_FF // side)]
        chunks += [(w2_hbm.at[layer, cols(c), :], w2_ref.at[cols(c), :]) for c in range(D_FF // side)]
        _stream_cast(chunks, stage_ref, sem)
        pre_buf[...] = jnp.zeros(pre_buf.shape, F32)

    ln1_g, ln1_b, ln2_g, ln2_b, sgln_g, sgln_b = (
        r[layer:layer + 1, :] for r in (ln1_g_ref, ln1_b_ref, ln2_g_ref, ln2_b_ref, sgln_g_ref, sgln_b_ref))
    tm = h_ref.shape[0]
    parts = [slice(r, r + OUT_ROWS) for r in range(0, tm, OUT_ROWS)]

    def previous_tile_out(after=None):
        token = None
        for rows in parts:
            pre = pre_buf[rows, :]
            if after is not None:
                pre = pre + jnp.tile(_runtime_zero(after), (OUT_ROWS // SUBLANES, D_MODEL // LANES))
            out = _layer_norm(pre, ln2_g, ln2_b)
            o_ref[rows, :] = out
            token = out[0:SUBLANES, 0:LANES] if token is None else token + out[0:SUBLANES, 0:LANES]
        return token

    @pl.when(t < n_tiles)
    def _():
        mix = []
        for rows in parts:
            ymix = ymix_ref[rows, :]
            mix.append(jnp.dot(ymix[:, 0:D_GROUP], w_out_ref[0:D_GROUP, :], preferred_element_type=F32)
                       + jnp.dot(ydiff_ref[rows, :], w_out_ref[D_GROUP:2 * D_GROUP, :],
                                 preferred_element_type=F32)
                       + jnp.dot(ymix[:, D_GROUP:2 * D_GROUP], w_out_ref[3 * D_GROUP:4 * D_GROUP, :],
                                 preferred_element_type=F32))
        y_sg = [_spatial_gating(gate_ref[rows, :], sgln_g, sgln_b, sg_w_ref[...], sg_bias_ref[...])
                for rows in parts]
        mix = [m + jnp.dot(sg, w_out_ref[2 * D_GROUP:3 * D_GROUP, :], preferred_element_type=F32)
               for m, sg in zip(mix, y_sg)]
        h1 = [_layer_norm(ALPHA * h_ref[rows, :] + m, ln1_g, ln1_b) for rows, m in zip(parts, mix)]
        act = []
        for x in h1:
            a = jnp.maximum(jnp.dot(x.astype(BF16), w1_ref[...], preferred_element_type=F32), 0.0)
            act.append((a * a).astype(BF16))
        token = previous_tile_out(after=a[0:SUBLANES, 0:LANES])
        ff = [jnp.dot(a, w2_ref[...], preferred_element_type=F32) for a in act]
        ff[-1] = ff[-1] + jnp.tile(_runtime_zero(token), (OUT_ROWS // SUBLANES, D_MODEL // LANES))
        for rows, x, f in zip(parts, h1, ff):
            pre_buf[rows, :] = ALPHA * x + f

    @pl.when(t == n_tiles)
    def _():
        previous_tile_out()


def _mixer_out(layer, h, ymix, ydiff, gate, weights, stacked, vectors):
    T = h.shape[0]
    tm = TM_OUT
    n_tiles = T // tm
    tok = lambda width: pl.BlockSpec((tm, width), lambda t: (jnp.minimum(t, n_tiles - 1), 0))
    assert D_MODEL == WEIGHT_CHUNK and D_FF % WEIGHT_CHUNK == 0 and OUT_ROWS % CHUNK == 0
    return pl.pallas_call(
        functools.partial(_mixer_out_kernel, layer),
        grid=(n_tiles + 1,),
        in_specs=([tok(D_MODEL), tok(2 * D_GROUP), tok(D_GROUP), tok(2 * D_GROUP)]
                  + [pl.BlockSpec(memory_space=pl.ANY) for _ in weights]
                  + [_layer_spec(p, layer) for p in stacked]
                  + [_const_spec(p.shape) for p in vectors]),
        out_specs=pl.BlockSpec((tm, D_MODEL), lambda t: (jnp.maximum(t - 1, 0), 0)),
        out_shape=jax.ShapeDtypeStruct((T, D_MODEL), F32),
        scratch_shapes=[pltpu.VMEM((D_MODEL, D_MODEL), BF16),
                        pltpu.VMEM((D_MODEL, D_FF), BF16),
                        pltpu.VMEM((D_FF, D_MODEL), BF16),
                        pltpu.VMEM((2, WEIGHT_CHUNK, WEIGHT_CHUNK), F32),
                        pltpu.SemaphoreType.DMA((2,)),
                        pltpu.VMEM((tm, D_MODEL), F32)],
        compiler_params=pltpu.CompilerParams(dimension_semantics=("arbitrary",),
                                             vmem_limit_bytes=VMEM_LIMIT_BYTES),
        name="mixer_out",
    )(h, ymix, ydiff, gate, *weights, *stacked, *vectors)


def _t5_bucket(n):
    max_exact = N_BUCKETS // 2
    large = max_exact + (jnp.log(jnp.maximum(n, 1).astype(F32) / max_exact)
                         / math.log(MAX_DISTANCE / max_exact) * (N_BUCKETS - max_exact)).astype(jnp.int32)
    large = jnp.minimum(large, N_BUCKETS - 1)
    return jnp.where(n < max_exact, n, large)


def _bias_table(rel_bias):
    assert TK >= MAX_DISTANCE, "tiles two or more behind the diagonal must see one constant bias"
    dist = jnp.arange(TQ + TK)
    onehot = (_t5_bucket(dist)[:, None] == jnp.arange(N_BUCKETS)[None, :]).astype(F32)
    rb = rel_bias.astype(F32)
    vals = jnp.sum(onehot[:, :, None] * (rb - rb[N_BUCKETS - 1])[None, :, :], axis=1) * LOG2E
    return jnp.concatenate([vals, jnp.full((TQ + TK, N_HEADS), MASK_VALUE, F32)], axis=0).T


def _block_diag(w):
    n_layers, g, c, d = w.shape
    out = jnp.zeros((n_layers, g * c, g * d), w.dtype)
    for i in range(g):
        out = out.at[:, i * c:(i + 1) * c, i * d:(i + 1) * d].set(w[:, i])
    return out


def kernel(x, emb_ln_g, emb_ln_b, rel_bias, w_in, conv_w, conv_b, conv_ln_g, conv_ln_b, conv_pw_w,
           conv_pw_b, lam_q1, lam_k1, lam_q2, lam_k2, diff_norm_g, sg_ln_g, sg_ln_b, sg_w, sg_b, pool_w,
           pool_scale, w_out, ln1_g, ln1_b, w_mlp1, w_mlp2, ln2_g, ln2_b):
    B, S, _ = x.shape
    assert S % TM_IN == 0 and (B * S) % TM_OUT == 0 and TQ == TK
    assert S % (Q_PER_STEP * TQ) == 0 and Q_PER_STEP % 2 == 0
    assert TM_OUT % OUT_ROWS == 0
    assert all(w.dtype == F32 for w in (w_in, w_out, w_mlp1, w_mlp2)), "weights are staged as f32"
    small = [conv_b, conv_ln_g, conv_ln_b, conv_pw_b, sg_ln_g, sg_ln_b, pool_scale, lam_q1, lam_k1,
             lam_q2, lam_k2, diff_norm_g, ln1_g, ln1_b, ln2_g, ln2_b]
    assert all(p.dtype == F32 and p.shape[0] == DEPTH and p.ndim == 2 for p in small)
    tril = jnp.tril(jnp.ones((CHUNK, CHUNK), F32))
    sg_w_cat = jnp.transpose(sg_w * tril, (0, 2, 1, 3)).reshape(DEPTH, CHUNK, N_SG_HEADS * CHUNK)
    sg_bias = jnp.repeat(jnp.swapaxes(sg_b, 1, 2), SG_HEAD_DIM, axis=2).astype(F32)
    in_matrices = [conv_w.astype(F32), conv_pw_w.astype(BF16), _block_diag(pool_w).astype(BF16)]
    in_vectors = [conv_b, conv_ln_g, conv_ln_b, conv_pw_b, pool_scale]
    attn_vectors = [lam_q1, lam_k1, lam_q2, lam_k2, jnp.tile(diff_norm_g, (1, N_HEADS))]
    out_weights = [w_out, w_mlp1, w_mlp2]
    out_matrices = [sg_w_cat.astype(BF16), sg_bias]
    out_vectors = [sg_ln_g, sg_ln_b, ln1_g, ln1_b, ln2_g, ln2_b]
    table = _bias_table(rel_bias)
    emb_g, emb_b = emb_ln_g.reshape(1, -1).astype(F32), emb_ln_b.reshape(1, -1).astype(F32)

    h = x
    for l in range(DEPTH):
        outs = _mixer_in(l, h, emb_g, emb_b, w_in, in_matrices, in_vectors)
        if l == 0:
            h, q, k, v, ymix, gate = outs
        else:
            q, k, v, ymix, gate = outs
        ydiff = _diff_attn(l, q, k, v, table, attn_vectors)
        h = _mixer_out(l, h.reshape(B * S, D_MODEL), ymix.reshape(B * S, 2 * D_GROUP),
                       ydiff.reshape(B * S, D_GROUP), gate.reshape(B * S, 2 * D_GROUP), out_weights,
                       out_matrices, out_vectors).reshape(B, S, D_MODEL)
    return h
```
